```python
import math
import jax, jax.numpy as jnp
from jax import lax
import numpy as np

D_MODEL = 2048
BATCH = 1
SEQ = 8192
DEPTH = 1
DEC_BATCH = 32
DEC_SEQ = 4
PAST_LEN = 8192
PAGE_SIZE = 128

H_A = 8
DH_A = 64
H_B = 8
DH_B = 128
H_I = 16
DH_I = 64
TOPK_MAX = 256
NUM_BUCKETS = 32
MAX_DISTANCE = 128
N_GROUPS = 4
EXPERTS_PER_GROUP = 8
N_EXPERTS = N_GROUPS * EXPERTS_PER_GROUP
TOP_K_IN_GROUP = 2
D_FF_EXPERT = 512
Q_BLOCK = 128
ALPHA = (2 * DEPTH) ** 0.25
BETA = (8 * DEPTH) ** -0.25
LN_EPS = 1e-5
IN_SIZES = (H_A * 2 * DH_A, H_A * 2 * DH_A, H_A * 2 * DH_A,
            H_B * DH_B, H_B * DH_B, H_B * DH_B,
            H_I * DH_I, DH_I, H_I, D_MODEL, D_MODEL)
IN_TOTAL = sum(IN_SIZES)

kernel_name = 'hybrid_diffattn_dsa_hmoe_step'


def layer_norm(x, g, b):
    xf = x.astype(jnp.float32)
    mu = jnp.mean(xf, axis=-1, keepdims=True)
    var = jnp.mean(jnp.square(xf - mu), axis=-1, keepdims=True)
    return ((xf - mu) * lax.rsqrt(var + LN_EPS) * g + b).astype(x.dtype)


def t5_bucket(rel):
    n = jnp.maximum(-rel, 0)
    max_exact = NUM_BUCKETS // 2
    nf = jnp.maximum(n, 1).astype(jnp.float32)
    large = max_exact + (jnp.log(nf / max_exact) / math.log(MAX_DISTANCE / max_exact)
                         * (NUM_BUCKETS - max_exact)).astype(jnp.int32)
    return jnp.where(n < max_exact, n, jnp.minimum(large, NUM_BUCKETS - 1))


def in_proj(x, w_in):
    B, T, _ = x.shape
    h = jnp.einsum('btd,de->bte', x, w_in)
    offs = np.cumsum(IN_SIZES)[:-1].tolist()
    qa, ka, va, qb, kb, vb, qi, ki, wi, ga, gb = jnp.split(h, offs, axis=-1)
    return (qa.reshape(B, T, H_A, 2 * DH_A), ka.reshape(B, T, H_A, 2 * DH_A), va.reshape(B, T, H_A, 2 * DH_A),
            qb.reshape(B, T, H_B, DH_B), kb.reshape(B, T, H_B, DH_B), vb.reshape(B, T, H_B, DH_B),
            qi.reshape(B, T, H_I, DH_I), ki, wi, ga, gb)


def diff_attention(q, k, v, q_pos, k_pos, lam, lambda_init, subln_w, bias_a):
    rel = k_pos[None, :] - q_pos[:, None]
    bias = bias_a[:, t5_bucket(rel)]
    mask = rel <= 0
    scale = DH_A ** -0.5

    def attn_map(qx, kx):
        s = jnp.einsum('bqhd,bshd->bhqs', qx, kx).astype(jnp.float32) * scale + bias
        return jax.nn.softmax(jnp.where(mask, s, -jnp.inf), axis=-1)

    p = attn_map(q[..., :DH_A], k[..., :DH_A]) - lam * attn_map(q[..., DH_A:], k[..., DH_A:])
    o = jnp.einsum('bhqs,bshd->bqhd', p.astype(v.dtype), v).astype(jnp.float32)
    o = o * lax.rsqrt(jnp.mean(o * o, axis=-1, keepdims=True) + LN_EPS) * subln_w * (1.0 - lambda_init)
    B, Tq = q.shape[:2]
    return o.astype(v.dtype).reshape(B, Tq, H_A * 2 * DH_A)


def indexer_select(qi, wi, ki, q_pos, k_pos, k_top):
    s = jnp.einsum('bqhd,bsd->bqhs', qi, ki).astype(jnp.float32) * DH_I ** -0.5
    score = jnp.einsum('bqh,bqhs->bqs', wi.astype(jnp.float32) * H_I ** -0.5, jax.nn.relu(s))
    score = jnp.where(k_pos[None, None, :] <= q_pos[None, :, None], score, -jnp.inf)
    _, idx = lax.top_k(score, k_top)
    return idx


def dsa_attend(q, kg, vg, q_pos, sel_pos, bias_b):
    rel = sel_pos - q_pos[None, :, None]
    valid = rel <= 0
    bias = jnp.moveaxis(bias_b[:, t5_bucket(rel)], 0, 1)
    s = jnp.einsum('bqhd,bqkhd->bhqk', q, kg).astype(jnp.float32) * DH_B ** -0.5 + bias
    p = jax.nn.softmax(jnp.where(valid[:, None], s, -jnp.inf), axis=-1)
    o = jnp.einsum('bhqk,bqkhd->bqhd', p.astype(vg.dtype), vg)
    B, Tq = q.shape[:2]
    return o.reshape(B, Tq, H_B * DH_B)


def merge_out(oa, ob, ga, gb, w_br_a, w_br_b, w_o):
    merged = jax.nn.sigmoid(ga) * (oa @ w_br_a) + jax.nn.sigmoid(gb) * (ob @ w_br_b)
    return merged @ w_o


def hier_moe(x, w_rg, b_rg, w_re, b_re, w_gate, w_up, w_down):
    g_logits = (x @ w_rg).astype(jnp.float32) + b_rg
    g_prob = jax.nn.softmax(g_logits, axis=-1)
    g_sel = jnp.argmax(g_logits, axis=-1)
    p_g = jnp.take_along_axis(g_prob, g_sel[:, None], axis=-1)
    e_logits = ((x @ w_re).astype(jnp.float32) + b_re).reshape(-1, N_GROUPS, EXPERTS_PER_GROUP)
    e_in_group = jnp.take_along_axis(e_logits, g_sel[:, None, None], axis=1)[:, 0]
    top_v, top_i = lax.top_k(e_in_group, TOP_K_IN_GROUP)
    w = jax.nn.softmax(top_v, axis=-1) * p_g
    eid = g_sel[:, None] * EXPERTS_PER_GROUP + top_i
    combine = jnp.sum(jax.nn.one_hot(eid, N_EXPERTS, dtype=jnp.float32) * w[..., None], axis=1)
    h = jax.nn.silu(jnp.einsum('td,edf->tef', x, w_gate)) * jnp.einsum('td,edf->tef', x, w_up)
    h = h * combine[..., None].astype(h.dtype)
    return jnp.einsum('tef,efd->td', h, w_down)


def setup_inputs(seed: int = 0) -> dict:
    key = jax.random.key(seed)
    ks = jax.random.split(key, 30)
    n_pages = PAST_LEN // PAGE_SIZE
    n_used = DEC_BATCH * n_pages
    n_phys = n_used + max(1, n_used // 4)
    d = D_MODEL

    def nrm(k, shape, s=1.0):
        return jax.random.normal(k, shape, jnp.float32) * s

    page_table = jax.random.permutation(ks[7], n_phys)[:n_used].reshape(DEC_BATCH, n_pages).astype(jnp.int32)
    return {
        'x_prompt': nrm(ks[0], (BATCH, SEQ, d)),
        'x_sample': nrm(ks[1], (DEC_BATCH, DEC_SEQ, d)),
        'cache_k_a': nrm(ks[2], (DEPTH, n_phys, PAGE_SIZE, H_A, 2 * DH_A)),
        'cache_v_a': nrm(ks[3], (DEPTH, n_phys, PAGE_SIZE, H_A, 2 * DH_A)),
        'cache_k_b': nrm(ks[4], (DEPTH, n_phys, PAGE_SIZE, H_B, DH_B)),
        'cache_v_b': nrm(ks[5], (DEPTH, n_phys, PAGE_SIZE, H_B, DH_B)),
        'cache_k_idx': nrm(ks[6], (DEPTH, n_phys, PAGE_SIZE, DH_I)),
        'page_table': page_table,
        'w_in': nrm(ks[8], (DEPTH, d, IN_TOTAL), d ** -0.5),
        'lambda_q1': nrm(ks[9], (DEPTH, DH_A), 0.1),
        'lambda_k1': nrm(ks[10], (DEPTH, DH_A), 0.1),
        'lambda_q2': nrm(ks[11], (DEPTH, DH_A), 0.1),
        'lambda_k2': nrm(ks[12], (DEPTH, DH_A), 0.1),
        'subln_w': 1.0 + nrm(ks[13], (DEPTH, 2 * DH_A), 0.02),
        'w_br_a': nrm(ks[14], (DEPTH, H_A * 2 * DH_A, d), (H_A * 2 * DH_A) ** -0.5),
        'w_br_b': nrm(ks[15], (DEPTH, H_B * DH_B, d), (H_B * DH_B) ** -0.5),
        'w_o': nrm(ks[16], (DEPTH, d, d), BETA * d ** -0.5),
        'rel_bias': nrm(ks[17], (NUM_BUCKETS, H_A + H_B), 0.5),
        'ln1_g': 1.0 + nrm(ks[18], (DEPTH, d), 0.02),
        'ln1_b': nrm(ks[19], (DEPTH, d), 0.01),
        'w_router_group': nrm(ks[20], (DEPTH, d, N_GROUPS), d ** -0.5),
        'b_router_group': nrm(ks[21], (DEPTH, N_GROUPS), 0.01),
        'w_router_expert': nrm(ks[22], (DEPTH, d, N_EXPERTS), d ** -0.5),
        'b_router_expert': nrm(ks[23], (DEPTH, N_EXPERTS), 0.01),
        'w_e_gate': nrm(ks[24], (DEPTH, N_EXPERTS, d, D_FF_EXPERT), d ** -0.5),
        'w_e_up': nrm(ks[25], (DEPTH, N_EXPERTS, d, D_FF_EXPERT), d ** -0.5),
        'w_e_down': nrm(ks[26], (DEPTH, N_EXPERTS, D_FF_EXPERT, d), BETA * D_FF_EXPERT ** -0.5),
        'ln2_g': 1.0 + nrm(ks[27], (DEPTH, d), 0.02),
        'ln2_b': nrm(ks[28], (DEPTH, d), 0.01),
    }


def reference(x_prompt, x_sample, cache_k_a, cache_v_a, cache_k_b, cache_v_b, cache_k_idx, page_table,
              w_in, lambda_q1, lambda_k1, lambda_q2, lambda_k2, subln_w, w_br_a, w_br_b, w_o, rel_bias,
              ln1_g, ln1_b, w_router_group, b_router_group, w_router_expert, b_router_expert,
              w_e_gate, w_e_up, w_e_down, ln2_g, ln2_b):
    B, T = x_prompt.shape[:2]
    Bs, Ts = x_sample.shape[:2]
    n_blocks = T // Q_BLOCK
    L_s = PAST_LEN + Ts
    k_top_p = min(TOPK_MAX, T // 4)
    k_top_s = min(TOPK_MAX, L_s // 4)
    pos_p = jnp.arange(T, dtype=jnp.int32)
    pos_s_k = jnp.arange(L_s, dtype=jnp.int32)
    pos_s_q = PAST_LEN + jnp.arange(Ts, dtype=jnp.int32)
    bias_a = rel_bias[:, :H_A].T
    bias_b = rel_bias[:, H_A:].T
    bi_p = jnp.arange(B)[:, None, None]
    bi_s = jnp.arange(Bs)[:, None, None]

    def to_blocks(a):
        return jnp.swapaxes(a.reshape((B, n_blocks, Q_BLOCK) + a.shape[2:]), 0, 1)

    def from_blocks(a):
        return jnp.swapaxes(a, 0, 1).reshape((B, T) + a.shape[3:])

    xp, xs = x_prompt, x_sample
    rows = [[] for _ in range(10)]
    for l in range(DEPTH):
        lambda_init = 0.8 - 0.6 * math.exp(-0.3 * l)
        lam = (jnp.exp(jnp.sum(lambda_q1[l].astype(jnp.float32) * lambda_k1[l].astype(jnp.float32)))
               - jnp.exp(jnp.sum(lambda_q2[l].astype(jnp.float32) * lambda_k2[l].astype(jnp.float32)))
               + lambda_init)

        qa, ka, va, qb, kb, vb, qi, ki, wi, ga, gb = in_proj(xp, w_in[l])

        def block_fn(blk):
            qa_b, qb_b, qi_b, wi_b, pos_b = blk
            oa_b = diff_attention(qa_b, ka, va, pos_b, pos_p, lam, lambda_init, subln_w[l], bias_a)
            idx = indexer_select(qi_b, wi_b, ki, pos_b, pos_p, k_top_p)
            ob_b = dsa_attend(qb_b, kb[bi_p, idx], vb[bi_p, idx], pos_b, idx, bias_b)
            return oa_b, ob_b

        oa_p, ob_p = lax.map(block_fn, (to_blocks(qa), to_blocks(qb), to_blocks(qi), to_blocks(wi),
                                        pos_p.reshape(n_blocks, Q_BLOCK)))
        mix_p = merge_out(from_blocks(oa_p), from_blocks(ob_p), ga, gb, w_br_a[l], w_br_b[l], w_o[l])
        for r, a in zip(rows[:5], (ka, va, kb, vb, ki)):
            r.append(a)

        qa_s, ka_s, va_s, qb_s, kb_s, vb_s, qi_s, ki_s, wi_s, ga_s, gb_s = in_proj(xs, w_in[l])

        def paged_all(cache, new):
            past = cache[l, page_table]
            past = past.reshape((Bs, PAST_LEN) + cache.shape[3:])
            return jnp.concatenate([past, new], axis=1)

        oa_s = diff_attention(qa_s, paged_all(cache_k_a, ka_s), paged_all(cache_v_a, va_s),
                              pos_s_q, pos_s_k, lam, lambda_init, subln_w[l], bias_a)
        idx_s = indexer_select(qi_s, wi_s, paged_all(cache_k_idx, ki_s), pos_s_q, pos_s_k, k_top_s)
        past_i = jnp.minimum(idx_s, PAST_LEN - 1)
        phys = page_table[bi_s, past_i // PAGE_SIZE]
        off = past_i % PAGE_SIZE
        cur_i = jnp.clip(idx_s - PAST_LEN, 0, Ts - 1)
        is_new = (idx_s >= PAST_LEN)[..., None, None]

        def gather_sel(cache, new):
            return jnp.where(is_new, new[bi_s, cur_i], cache[l, phys, off])

        ob_s = dsa_attend(qb_s, gather_sel(cache_k_b, kb_s), gather_sel(cache_v_b, vb_s), pos_s_q, idx_s, bias_b)
        mix_s = merge_out(oa_s, ob_s, ga_s, gb_s, w_br_a[l], w_br_b[l], w_o[l])
        for r, a in zip(rows[5:], (ka_s, va_s, kb_s, vb_s, ki_s)):
            r.append(a)

        x1p = layer_norm(ALPHA * xp + mix_p, ln1_g[l], ln1_b[l])
        x1s = layer_norm(ALPHA * xs + mix_s, ln1_g[l], ln1_b[l])
        tok = jnp.concatenate([x1p.reshape(-1, D_MODEL), x1s.reshape(-1, D_MODEL)], axis=0)
        f = hier_moe(tok, w_router_group[l], b_router_group[l], w_router_expert[l], b_router_expert[l],
                     w_e_gate[l], w_e_up[l], w_e_down[l])
        f_p = f[:B * T].reshape(B, T, D_MODEL)
        f_s = f[B * T:].reshape(Bs, Ts, D_MODEL)
        xp = layer_norm(ALPHA * x1p + f_p, ln2_g[l], ln2_b[l])
        xs = layer_norm(ALPHA * x1s + f_s, ln2_g[l], ln2_b[l])

    st = [jnp.stack(r, axis=0) for r in rows]
    return (xp, xs, st[0], st[1], st[2], st[3], st[4], st[5], st[6], st[7], st[8], st[9])
```

```python
import functools
import math

import numpy as np
import jax
import jax.numpy as jnp
from jax import lax
from jax.experimental import pallas as pl
from jax.experimental.pallas import tpu as pltpu

F32 = jnp.float32
BF16 = jnp.bfloat16
I32 = jnp.int32

H_A = 8
DH_A = 64
H_B = 8
DH_B = 128
H_I = 16
DH_I = 64
TOPK_MAX = 256
NUM_BUCKETS = 32
MAX_DISTANCE = 128
N_GROUPS = 4
EXPERTS_PER_GROUP = 8
N_EXPERTS = N_GROUPS * EXPERTS_PER_GROUP
D_FF_EXPERT = 512
PAGE_SIZE = 128
DEPTH = 1
ALPHA = (2 * DEPTH) ** 0.25
LN_EPS = 1e-5
LAMBDA_INIT = 0.8 - 0.6 * math.exp(-0.3 * 0)

LANES = 128
NEG = -1e30
INT_MIN = -2 ** 31
KEY_NEG_INF = int(np.array([-np.inf], np.float32).view(np.int32)[0]) ^ 0x7FFFFFFF
V7X_VMEM_LIMIT = 48 * 1024 * 1024

PROJ_TM = 512
FLASH_T = 512
SEL_TQ = 256
SEL_CH = 512
SEL_RS = 64
POST_TM = 256
MOE_TM = 256


def _t5_thresholds():
    n = np.arange(0, MAX_DISTANCE + 1)
    max_exact = NUM_BUCKETS // 2
    nf = np.maximum(n, 1).astype(np.float32)
    large = max_exact + (np.log(nf / max_exact) / math.log(MAX_DISTANCE / max_exact)
                         * (NUM_BUCKETS - max_exact)).astype(np.int32)
    b = np.where(n < max_exact, n, np.minimum(large, NUM_BUCKETS - 1))
    assert np.all(np.diff(b) >= 0) and b[-1] == NUM_BUCKETS - 1
    return tuple(int(np.argmax(b >= j)) for j in range(1, NUM_BUCKETS))


T5_THRESH = _t5_thresholds()


def _t5_bias(d, tab_ref, h):
    b = jnp.full(d.shape, tab_ref[h, 0], F32)
    for j, t in enumerate(T5_THRESH, start=1):
        b = jnp.where(d >= t, tab_ref[h, j], b)
    return b


def _dot_nt(a, b):
    return lax.dot_general(a, b, (((1,), (1,)), ((), ())), preferred_element_type=F32)


def _sort_key(x):
    bits = lax.bitcast_convert_type(x, I32)
    return jnp.where(bits < 0, bits ^ 0x7FFFFFFF, bits)


def _params(sem):
    return pltpu.CompilerParams(dimension_semantics=sem, vmem_limit_bytes=V7X_VMEM_LIMIT)


def _proj_kernel(x_ref, w_ref, *out_refs, emit):
    res = jnp.dot(x_ref[...], w_ref[...], preferred_element_type=F32)
    emit(res, out_refs)


def _emit_f32(res, outs):
    outs[0][...] = res


def _emit_bf16(res, outs):
    outs[0][...] = res.astype(BF16)


def _emit_kv(res, outs):
    outs[0][...] = res
    for h in range(res.shape[1] // LANES):
        outs[1][h] = res[:, h * LANES:(h + 1) * LANES].astype(BF16)


def _emit_heads(res, outs):
    for h in range(res.shape[1] // LANES):
        outs[0][h] = res[:, h * LANES:(h + 1) * LANES].astype(BF16)


def _emit_qa(res, outs):
    lane = lax.broadcasted_iota(I32, (res.shape[0], LANES), 1)
    for h in range(res.shape[1] // LANES):
        blk = res[:, h * LANES:(h + 1) * LANES] * (DH_A ** -0.5)
        outs[0][h, 0] = jnp.where(lane < DH_A, blk, 0.0).astype(BF16)
        outs[0][h, 1] = jnp.where(lane >= DH_A, blk, 0.0).astype(BF16)


def _emit_small(res, outs):
    outs[0][...] = res
    outs[1][...] = res[:, :DH_I]
    outs[2][...] = res[:, :DH_I].astype(BF16)


def _proj(x, w, emit, out_shapes, out_blocks, name):
    m, k = x.shape
    n = w.shape[1]
    tm = min(PROJ_TM, m)
    out_specs = []
    for blk in out_blocks:
        nd = len(blk)
        tok_axis = nd - 2
        out_specs.append(pl.BlockSpec(blk, functools.partial(
            lambda i, nd, tok_axis: tuple(i if a == tok_axis else 0 for a in range(nd)), nd=nd, tok_axis=tok_axis)))
    return pl.pallas_call(
        functools.partial(_proj_kernel, emit=emit),
        grid=(m // tm,),
        in_specs=[pl.BlockSpec((tm, k), lambda i: (i, 0)), pl.BlockSpec((k, n), lambda i: (0, 0))],
        out_specs=out_specs,
        out_shape=out_shapes,
        compiler_params=_params(("arbitrary",)),
        name=name,
    )(x, w)


def _in_proj(x, w_bf, w_small_bf):
    m = x.shape[0]
    tm = min(PROJ_TM, m)
    hd = H_A * 2 * DH_A
    sds = jax.ShapeDtypeStruct
    cols = lambda j: w_bf[:, j * hd:(j + 1) * hd]
    qa, = _proj(x, cols(0), _emit_qa, [sds((H_A, 2, m, LANES), BF16)], [(H_A, 2, tm, LANES)], "proj_qa")
    kv = []
    for j, nm in ((1, "ka"), (2, "va"), (4, "kb"), (5, "vb")):
        kv.append(_proj(x, cols(j), _emit_kv, [sds((m, hd), F32), sds((H_A, m, LANES), BF16)],
                        [(tm, hd), (H_A, tm, LANES)], "proj_" + nm))
    qb, = _proj(x, cols(3), _emit_heads, [sds((H_B, m, LANES), BF16)], [(H_B, tm, LANES)], "proj_qb")
    qi, = _proj(x, cols(6), _emit_bf16, [sds((m, hd), BF16)], [(tm, hd)], "proj_qi")
    g0 = 7 * hd + DH_I + H_I
    d = x.shape[1]
    ga, = _proj(x, w_bf[:, g0:g0 + d], _emit_f32, [sds((m, d), F32)], [(tm, d)], "proj_ga")
    gb, = _proj(x, w_bf[:, g0 + d:g0 + 2 * d], _emit_f32, [sds((m, d), F32)], [(tm, d)], "proj_gb")
    small, ki, ki_bf = _proj(x, w_small_bf, _emit_small,
                             [sds((m, LANES), F32), sds((m, DH_I), F32), sds((m, DH_I), BF16)],
                             [(tm, LANES), (tm, DH_I), (tm, DH_I)], "proj_small")
    return dict(qa=qa, ka=kv[0], va=kv[1], kb=kv[2], vb=kv[3], qb=qb, qi=qi, ga=ga, gb=gb,
                small=small, ki=ki, ki_bf=ki_bf)


def _lambda_full(lam_ref):
    a = jnp.sum(lam_ref[0:1, :] * lam_ref[1:2, :], axis=1, keepdims=True)
    b = jnp.sum(lam_ref[2:3, :] * lam_ref[3:4, :], axis=1, keepdims=True)
    return jnp.exp(a) - jnp.exp(b) + LAMBDA_INIT


def _subln(o, sub_ref):
    return o * lax.rsqrt(jnp.mean(o * o, axis=1, keepdims=True) + LN_EPS) * sub_ref[...] * (1.0 - LAMBDA_INIT)


def _flash_kernel(qs_ref, ks_ref, tab_ref, lam_ref, sub_ref, q_ref, k_ref, v_ref, *rest,
                  n_maps, tq, use_mask, scale, head_off):
    if use_mask:
        m_ref, o_ref, acc, m_s, l_s, bias_s = rest
    else:
        o_ref, acc, m_s, l_s, bias_s = rest
    h = pl.program_id(0)
    step = pl.program_id(1)
    qi = qs_ref[step]
    ki = ks_ref[step]
    rows = n_maps * tq
    hb = h + head_off

    @pl.when(step == 0)
    def _():
        r = lax.broadcasted_iota(I32, (tq, tq), 0)
        c = lax.broadcasted_iota(I32, (tq, tq), 1)
        d0 = r - c
        bias_s[0] = jnp.where(d0 >= 0, _t5_bias(d0, tab_ref, hb), NEG)
        bias_s[1] = _t5_bias(d0 + tq, tab_ref, hb)

    @pl.when(ki == 0)
    def _():
        m_s[...] = jnp.full(m_s.shape, NEG, F32)
        l_s[...] = jnp.zeros(l_s.shape, F32)
        acc[...] = jnp.zeros(acc.shape, F32)

    def update(bias):
        q = q_ref[...].reshape(rows, LANES)
        s = _dot_nt(q, k_ref[...])
        if scale != 1.0:
            s = s * scale
        if n_maps == 2 and bias.ndim == 0:
            s = s + bias
        elif n_maps == 2:
            s = (s.reshape(2, tq, tq) + bias[None]).reshape(rows, tq)
        else:
            s = s + bias
        if use_mask:
            s = s + m_ref[...]
        m_old = m_s[...]
        m_new = jnp.maximum(m_old, jnp.max(s, axis=1, keepdims=True))
        alpha = jnp.exp(m_old - m_new)
        p = jnp.exp(s - m_new)
        l_s[...] = alpha * l_s[...] + jnp.sum(p, axis=1, keepdims=True)
        acc[...] = alpha * acc[...] + jnp.dot(p.astype(BF16), v_ref[...], preferred_element_type=F32)
        m_s[...] = m_new

    @pl.when(qi - ki >= 2)
    def _():
        update(tab_ref[hb, NUM_BUCKETS - 1])

    @pl.when(qi - ki < 2)
    def _():
        update(bias_s[qi - ki])

    @pl.when(ki == qi)
    def _():
        o = acc[...] / l_s[...]
        if n_maps == 2:
            o = o[:tq] - _lambda_full(lam_ref) * o[tq:]
            o = _subln(o, sub_ref)
        o_ref[...] = o.astype(BF16)


def _flash(q, k, v, mask, tab, lam4, subw, *, n_maps, scale, head_off, name):
    nh, t = k.shape[0], k.shape[1]
    tq = min(FLASH_T, t)
    nq = t // tq
    pairs = [(a, b) for a in range(nq) for b in range(a + 1)]
    qs = jnp.asarray([p[0] for p in pairs], I32)
    ks = jnp.asarray([p[1] for p in pairs], I32)
    rows = n_maps * tq
    smem = pl.BlockSpec(memory_space=pltpu.SMEM)
    if n_maps == 2:
        q_spec = pl.BlockSpec((None, 2, tq, LANES), lambda h, s, qs, ks: (h, 0, qs[s], 0))
    else:
        q_spec = pl.BlockSpec((None, tq, LANES), lambda h, s, qs, ks: (h, qs[s], 0))
    kv_spec = pl.BlockSpec((None, tq, LANES), lambda h, s, qs, ks: (h, ks[s], 0))
    in_specs = [smem, pl.BlockSpec((4, DH_A), lambda h, s, qs, ks: (0, 0)),
                pl.BlockSpec((1, LANES), lambda h, s, qs, ks: (0, 0)), q_spec, kv_spec, kv_spec]
    args = [tab, lam4, subw, q, k, v]
    if mask is not None:
        in_specs.append(pl.BlockSpec((None, tq, tq), lambda h, s, qs, ks: (ks[s], qs[s], 0)))
        args.append(mask)
    grid_spec = pltpu.PrefetchScalarGridSpec(
        num_scalar_prefetch=2,
        grid=(nh, len(pairs)),
        in_specs=in_specs,
        out_specs=pl.BlockSpec((tq, LANES), lambda h, s, qs, ks: (qs[s], h)),
        scratch_shapes=[pltpu.VMEM((rows, LANES), F32), pltpu.VMEM((rows, 1), F32), pltpu.VMEM((rows, 1), F32),
                        pltpu.VMEM((2, tq, tq), F32)],
    )
    return pl.pallas_call(
        functools.partial(_flash_kernel, n_maps=n_maps, tq=tq, use_mask=mask is not None, scale=scale,
                          head_off=head_off),
        grid_spec=grid_spec,
        out_shape=jax.ShapeDtypeStruct((t, nh * LANES), BF16),
        compiler_params=_params(("arbitrary", "arbitrary")),
        name=name,
    )(qs, ks, *args)


def _kth_largest_key(count_ge, shape, k_top):
    cand0 = jnp.zeros(shape, I32)
    res = jnp.where(count_ge(cand0) >= k_top, cand0, jnp.full(shape, INT_MIN, I32))

    def bit_body(b, res):
        cand = res + jnp.left_shift(jnp.int32(1), jnp.int32(30) - b)
        return jnp.where(count_ge(cand) >= k_top, cand, res)

    return lax.fori_loop(0, 31, bit_body, res)


def _index_select_kernel(qh_ref, w_ref, ki_ref, o_ref, keys_s, wb_s, thr_s, *, tq, ch, n_ch, k_top):
    i = pl.program_id(0)
    q0 = i * tq
    n_valid = (q0 + tq + ch - 1) // ch
    sub = ch // LANES

    for h in range(H_I):
        wb_s[h] = jnp.broadcast_to(w_ref[:, DH_I + h:DH_I + h + 1] * (DH_I ** -0.5 * H_I ** -0.5), (tq, LANES))

    row = lax.broadcasted_iota(I32, (tq, LANES), 0) + q0
    lane = lax.broadcasted_iota(I32, (tq, LANES), 1)

    def score_chunk(c, _):
        k0 = pl.multiple_of(c * ch, ch)
        kc = ki_ref[pl.ds(k0, ch), :]
        acc = [jnp.zeros((tq, LANES), F32) for _ in range(sub)]
        for h in range(H_I):
            s = jnp.maximum(_dot_nt(qh_ref[h], kc), 0.0)
            wb = wb_s[h]
            for j in range(sub):
                acc[j] = acc[j] + wb * s[:, j * LANES:(j + 1) * LANES]
        for j in range(sub):
            kpos = k0 + j * LANES + lane
            val = jnp.where(kpos <= row, acc[j], -jnp.inf)
            keys_s[c, :, j * LANES:(j + 1) * LANES] = _sort_key(val)
        return 0

    lax.fori_loop(0, n_valid, score_chunk, 0)

    def search_rows(rb, _):
        r0 = pl.multiple_of(rb * SEL_RS, SEL_RS)

        def count_ge(cand):
            def chunk(c, a):
                blk = keys_s[c, pl.ds(r0, SEL_RS), :]
                for j in range(sub):
                    a = a + (blk[:, j * LANES:(j + 1) * LANES] >= cand).astype(I32)
                return a
            a = lax.fori_loop(0, n_valid, chunk, jnp.zeros((SEL_RS, LANES), I32))
            return jnp.sum(a.astype(F32), axis=1, keepdims=True)

        thr_s[pl.ds(r0, SEL_RS), :] = _kth_largest_key(count_ge, (SEL_RS, LANES), k_top)
        return 0

    lax.fori_loop(0, tq // SEL_RS, search_rows, 0)

    def write_chunk(c, _):
        thr = thr_s[...]
        for j in range(sub):
            key = keys_s[c, :, j * LANES:(j + 1) * LANES]
            sel = (key >= thr) & (key > KEY_NEG_INF)
            o_ref[c, :, j * LANES:(j + 1) * LANES] = jnp.where(sel, 0.0, NEG)
        return 0

    def fill_chunk(c, _):
        o_ref[c] = jnp.full((tq, ch), NEG, F32)
        return 0

    lax.fori_loop(0, n_valid, write_chunk, 0)
    lax.fori_loop(n_valid, n_ch, fill_chunk, 0)


def _index_select(qh, small, ki_bf, k_top):
    t = ki_bf.shape[0]
    tq = min(SEL_TQ, t)
    ch = min(SEL_CH, t)
    n_ch = t // ch
    return pl.pallas_call(
        functools.partial(_index_select_kernel, tq=tq, ch=ch, n_ch=n_ch, k_top=k_top),
        grid=(t // tq,),
        in_specs=[pl.BlockSpec((H_I, tq, DH_I), lambda i: (0, i, 0)),
                  pl.BlockSpec((tq, LANES), lambda i: (i, 0)),
                  pl.BlockSpec((t, DH_I), lambda i: (0, 0))],
        out_specs=pl.BlockSpec((n_ch, tq, ch), lambda i: (0, i, 0)),
        out_shape=jax.ShapeDtypeStruct((n_ch, t, ch), F32),
        scratch_shapes=[pltpu.VMEM((n_ch, tq, ch), I32), pltpu.VMEM((H_I, tq, LANES), F32),
                        pltpu.VMEM((tq, LANES), I32)],
        compiler_params=_params(("arbitrary",)),
        name="index_select",
    )(qh, small, ki_bf)


SROWS = 8


def _sample_select_kernel(pt_ref, q_ref, wb_ref, kc_ref, kn_ref, o_ref, keys_s, *, n_pages, n_new, k_top):
    p = pl.program_id(1)
    row = lax.broadcasted_iota(I32, (SROWS, LANES), 0)
    lane = lax.broadcasted_iota(I32, (SROWS, LANES), 1)
    tok = row % n_new

    def score(k_f32):
        s = jnp.maximum(_dot_nt(q_ref[...], k_f32.astype(BF16)), 0.0) * wb_ref[...]
        return jnp.sum(s.reshape(H_I, SROWS, LANES), axis=0)

    @pl.when(p < n_pages)
    def _():
        keys_s[p] = _sort_key(score(kc_ref[...]))

    @pl.when(p == n_pages)
    def _():
        val = jnp.where((lane <= tok) & (lane < n_new), score(kn_ref[...]), -jnp.inf)
        keys_s[p] = _sort_key(val)

        def count_ge(cand):
            a = lax.fori_loop(0, n_pages + 1, lambda c, a: a + (keys_s[c] >= cand).astype(I32),
                              jnp.zeros((SROWS, LANES), I32))
            return jnp.sum(a.astype(F32), axis=1, keepdims=True)

        thr = _kth_largest_key(count_ge, (SROWS, LANES), k_top)

        def write(c, _):
            key = keys_s[c]
            o_ref[c] = jnp.where((key >= thr) & (key > KEY_NEG_INF), 0.0, NEG)
            return 0

        lax.fori_loop(0, n_pages + 1, write, 0)


def _sample_select(page_table, qh, wb, cache_k_idx, ki_new, k_top):
    nb, n_pages = page_table.shape
    n_new = 4
    grid_spec = pltpu.PrefetchScalarGridSpec(
        num_scalar_prefetch=1,
        grid=(nb, n_pages + 1),
        in_specs=[pl.BlockSpec((None, H_I * SROWS, DH_I), lambda b, p, pt: (b, 0, 0)),
                  pl.BlockSpec((None, H_I * SROWS, LANES), lambda b, p, pt: (b, 0, 0)),
                  pl.BlockSpec((None, PAGE_SIZE, DH_I),
                               lambda b, p, pt: (pt[b, jnp.minimum(p, n_pages - 1)], 0, 0)),
                  pl.BlockSpec((None, PAGE_SIZE, DH_I), lambda b, p, pt: (b, 0, 0))],
        out_specs=pl.BlockSpec((None, n_pages + 1, SROWS, LANES), lambda b, p, pt: (b, 0, 0, 0)),
        scratch_shapes=[pltpu.VMEM((n_pages + 1, SROWS, LANES), I32)],
    )
    return pl.pallas_call(
        functools.partial(_sample_select_kernel, n_pages=n_pages, n_new=n_new, k_top=k_top),
        grid_spec=grid_spec,
        out_shape=jax.ShapeDtypeStruct((nb, n_pages + 1, SROWS, LANES), F32),
        compiler_params=_params(("arbitrary", "arbitrary")),
        name="sample_select",
    )(page_table, qh, wb, cache_k_idx, ki_new)


def _sample_attn_kernel(pt_ref, tab_ref, lam_ref, sub_ref, q_ref, kc_ref, vc_ref, kn_ref, vn_ref, *rest,
                        n_maps, use_mask, scale, head_off, n_pages, n_new, n_heads):
    if use_mask:
        m_ref, o_ref, acc, m_s, l_s = rest
    else:
        o_ref, acc, m_s, l_s = rest
    p = pl.program_id(1)
    past_len = n_pages * PAGE_SIZE
    row = lax.broadcasted_iota(I32, (SROWS, LANES), 0)
    lane = lax.broadcasted_iota(I32, (SROWS, LANES), 1)
    tok = row % n_new
    dist = (past_len + tok) - (p * PAGE_SIZE + lane)
    ok = (dist >= 0) & ((p < n_pages) | (lane < n_new))

    @pl.when(p == 0)
    def _():
        m_s[...] = jnp.full(m_s.shape, NEG, F32)
        l_s[...] = jnp.zeros(l_s.shape, F32)
        acc[...] = jnp.zeros(acc.shape, F32)

    def update(k_f32, v_f32):
        kb = k_f32.astype(BF16)
        vb = v_f32.astype(BF16)
        for h in range(n_heads):
            s = _dot_nt(q_ref[h], kb[:, h * LANES:(h + 1) * LANES])
            if scale != 1.0:
                s = s * scale
            s = jnp.where(ok, s + _t5_bias(dist, tab_ref, h + head_off), NEG)
            if use_mask:
                s = s + m_ref[...]
            m_old = m_s[h]
            m_new = jnp.maximum(m_old, jnp.max(s, axis=1, keepdims=True))
            alpha = jnp.exp(m_old - m_new)
            pr = jnp.exp(s - m_new)
            l_s[h] = alpha * l_s[h] + jnp.sum(pr, axis=1, keepdims=True)
            acc[h] = alpha * acc[h] + jnp.dot(pr.astype(BF16), vb[:, h * LANES:(h + 1) * LANES],
                                              preferred_element_type=F32)
            m_s[h] = m_new

    @pl.when(p < n_pages)
    def _():
        update(kc_ref[...], vc_ref[...])

    @pl.when(p == n_pages)
    def _():
        update(kn_ref[...], vn_ref[...])
        for h in range(n_heads):
            o = acc[h] / l_s[h]
            if n_maps == 2:
                o = o - _lambda_full(lam_ref) * pltpu.roll(o, n_new, 0)
                o = _subln(o, sub_ref)
            o_ref[:, h * LANES:(h + 1) * LANES] = o


def _sample_attn(page_table, tab, lam4, subw, q, cache_k, cache_v, k_new, v_new, mask, *,
                 n_maps, scale, head_off, name):
    nb, n_pages = page_table.shape
    nh = q.shape[1]
    hd = nh * LANES
    smem = pl.BlockSpec(memory_space=pltpu.SMEM)
    page_spec = pl.BlockSpec((None, PAGE_SIZE, hd), lambda b, p, pt: (pt[b, jnp.minimum(p, n_pages - 1)], 0, 0))
    new_spec = pl.BlockSpec((None, PAGE_SIZE, hd), lambda b, p, pt: (b, 0, 0))
    in_specs = [smem, pl.BlockSpec((4, DH_A), lambda b, p, pt: (0, 0)),
                pl.BlockSpec((1, LANES), lambda b, p, pt: (0, 0)),
                pl.BlockSpec((None, nh, SROWS, LANES), lambda b, p, pt: (b, 0, 0, 0)),
                page_spec, page_spec, new_spec, new_spec]
    args = [tab, lam4, subw, q, cache_k, cache_v, k_new, v_new]
    if mask is not None:
        in_specs.append(pl.BlockSpec((None, None, SROWS, LANES), lambda b, p, pt: (b, p, 0, 0)))
        args.append(mask)
    grid_spec = pltpu.PrefetchScalarGridSpec(
        num_scalar_prefetch=1,
        grid=(nb, n_pages + 1),
        in_specs=in_specs,
        out_specs=pl.BlockSpec((None, SROWS, hd), lambda b, p, pt: (b, 0, 0)),
        scratch_shapes=[pltpu.VMEM((nh, SROWS, LANES), F32), pltpu.VMEM((nh, SROWS, 1), F32),
                        pltpu.VMEM((nh, SROWS, 1), F32)],
    )
    return pl.pallas_call(
        functools.partial(_sample_attn_kernel, n_maps=n_maps, use_mask=mask is not None, scale=scale,
                          head_off=head_off, n_pages=n_pages, n_new=4, n_heads=nh),
        grid_spec=grid_spec,
        out_shape=jax.ShapeDtypeStruct((nb, SROWS, hd), F32),
        compiler_params=_params(("arbitrary", "arbitrary")),
        name=name,
    )(page_table, *args)


def _gate_merge_kernel(oa_ref, ob_ref, ga_ref, gb_ref, wa_ref, wb_ref, o_ref):
    a = jnp.dot(oa_ref[...], wa_ref[...], preferred_element_type=F32)
    b = jnp.dot(ob_ref[...], wb_ref[...], preferred_element_type=F32)
    o_ref[...] = (jax.nn.sigmoid(ga_ref[...]) * a + jax.nn.sigmoid(gb_ref[...]) * b).astype(BF16)


def _gate_merge(oa, ob, ga, gb, wa, wb):
    m, d = ga.shape
    tm = min(POST_TM, m)
    row = lambda w: pl.BlockSpec((tm, w), lambda i: (i, 0))
    full = lambda a: pl.BlockSpec(a.shape, lambda i: (0, 0))
    return pl.pallas_call(
        _gate_merge_kernel,
        grid=(m // tm,),
        in_specs=[row(oa.shape[1]), row(ob.shape[1]), row(d), row(d), full(wa), full(wb)],
        out_specs=row(d),
        out_shape=jax.ShapeDtypeStruct((m, d), BF16),
        compiler_params=_params(("arbitrary",)),
        name="gate_merge",
    )(oa, ob, ga, gb, wa, wb)


def _store_token_major(ref, x):
    tm, d = x.shape
    nc = d // LANES
    for c in range(nc):
        ref[pl.ds(c, tm, stride=nc), :] = x[:, c * LANES:(c + 1) * LANES]


def _load_token_major(ref, tm, nc, dtype):
    return jnp.concatenate([ref[pl.ds(c, tm, stride=nc), :].astype(dtype) for c in range(nc)], axis=1)


def _layer_norm(x, g_ref, b_ref):
    mu = jnp.mean(x, axis=1, keepdims=True)
    xc = x - mu
    var = jnp.mean(xc * xc, axis=1, keepdims=True)
    return xc * lax.rsqrt(var + LN_EPS) * g_ref[...] + b_ref[...]


def _out_ln_route_kernel(mg_ref, x_ref, wo_ref, g_ref, b_ref, wr_ref, br_ref, cin_ref,
                         x1t_ref, route_ref, cnt_ref, carry, *, tm):
    i = pl.program_id(0)

    @pl.when(i == 0)
    def _():
        carry[...] = cin_ref[...]

    mix = jnp.dot(mg_ref[...], wo_ref[...], preferred_element_type=F32)
    x1 = _layer_norm(ALPHA * x_ref[...] + mix, g_ref, b_ref)
    _store_token_major(x1t_ref, x1)

    z = jnp.dot(x1, wr_ref[...], preferred_element_type=F32, precision=lax.Precision.HIGHEST) + br_ref[...]
    lane = lax.broadcasted_iota(I32, (tm, LANES), 1).astype(F32)
    ninf = -jnp.inf
    big = jnp.float32(2 ** 30)
    gl = jnp.where(lane < N_GROUPS, z, ninf)
    gmax = jnp.max(gl, axis=1, keepdims=True)
    gsel = jnp.min(jnp.where(gl == gmax, lane, big), axis=1, keepdims=True)
    p_g = 1.0 / jnp.sum(jnp.exp(gl - gmax), axis=1, keepdims=True)
    e_lane = lane - N_GROUPS
    in_grp = (e_lane >= gsel * EXPERTS_PER_GROUP) & (e_lane < (gsel + 1) * EXPERTS_PER_GROUP)
    el = jnp.where(in_grp, z, ninf)
    v1 = jnp.max(el, axis=1, keepdims=True)
    i1 = jnp.min(jnp.where(el == v1, lane, big), axis=1, keepdims=True)
    el2 = jnp.where(lane == i1, ninf, el)
    v2 = jnp.max(el2, axis=1, keepdims=True)
    i2 = jnp.min(jnp.where(el2 == v2, lane, big), axis=1, keepdims=True)
    e2x = jnp.exp(v2 - v1)
    w1 = p_g / (1.0 + e2x)
    w2 = p_g * e2x / (1.0 + e2x)
    e1 = i1 - N_GROUPS
    e2 = i2 - N_GROUPS

    hot1 = lane == e1
    hot2 = lane == e2
    onehot = (hot1 | hot2).astype(BF16)
    r = lax.broadcasted_iota(I32, (tm, tm), 0)
    c = lax.broadcasted_iota(I32, (tm, tm), 1)
    tri = (r > c).astype(BF16)
    prefix = jnp.dot(tri, onehot, preferred_element_type=F32) + carry[...]
    rank1 = jnp.sum(jnp.where(hot1, prefix, 0.0), axis=1, keepdims=True)
    rank2 = jnp.sum(jnp.where(hot2, prefix, 0.0), axis=1, keepdims=True)
    carry[...] = carry[...] + jnp.sum(onehot.astype(F32), axis=0, keepdims=True)
    cnt_ref[...] = carry[...]

    cols = (e1.astype(F32), e2.astype(F32), w1, w2, rank1, rank2)
    route = jnp.zeros((tm, LANES), F32)
    for j, col in enumerate(cols):
        route = jnp.where(lane == j, col, route)
    route_ref[...] = route


def _out_ln_route(merged, x, wo, g, b, wr, br, carry_in):
    m, d = x.shape
    tm = min(POST_TM, m)
    nc = d // LANES
    row = lambda w: pl.BlockSpec((tm, w), lambda i: (i, 0))
    full = lambda a: pl.BlockSpec(a.shape, lambda i: (0, 0))
    sds = jax.ShapeDtypeStruct
    return pl.pallas_call(
        functools.partial(_out_ln_route_kernel, tm=tm),
        grid=(m // tm,),
        in_specs=[row(d), row(d), full(wo), full(g), full(b), full(wr), full(br), full(carry_in)],
        out_specs=[pl.BlockSpec((tm * nc, LANES), lambda i: (i, 0)), row(LANES),
                   pl.BlockSpec((1, LANES), lambda i: (0, 0))],
        out_shape=[sds((m * nc, LANES), F32), sds((m, LANES), F32), sds((1, LANES), F32)],
        scratch_shapes=[pltpu.VMEM((1, LANES), F32)],
        compiler_params=_params(("arbitrary",)),
        name="out_ln_route",
    )(merged, x, wo, g, b, wr, br, carry_in)


def _dispatch_kernel(pos1_ref, pos2_ref, x_hbm, xs_in, xs_out, sem, *, tm, nc):
    del xs_in
    base = pl.program_id(0) * tm

    def row_copy(r, pos_ref, slot):
        t = base + r
        src = x_hbm.at[pl.ds(pl.multiple_of(t * nc, nc), nc)]
        dst = xs_out.at[pl.ds(pl.multiple_of(pos_ref[t] * nc, nc), nc)]
        return pltpu.make_async_copy(src, dst, sem.at[slot])

    def start(r, _):
        row_copy(r, pos1_ref, 0).start()
        row_copy(r, pos2_ref, 1).start()
        return 0

    def wait(r, _):
        row_copy(r, pos1_ref, 0).wait()
        row_copy(r, pos2_ref, 1).wait()
        return 0

    lax.fori_loop(0, tm, start, 0)
    lax.fori_loop(0, tm, wait, 0)


def _dispatch(pos1, pos2, x1t, xs, nc):
    m = x1t.shape[0] // nc
    tm = min(POST_TM, m)
    any_spec = pl.BlockSpec(memory_space=pl.ANY)
    grid_spec = pltpu.PrefetchScalarGridSpec(
        num_scalar_prefetch=2, grid=(m // tm,), in_specs=[any_spec, any_spec], out_specs=any_spec,
        scratch_shapes=[pltpu.SemaphoreType.DMA((2,))])
    return pl.pallas_call(
        functools.partial(_dispatch_kernel, tm=tm, nc=nc),
        grid_spec=grid_spec,
        out_shape=jax.ShapeDtypeStruct(xs.shape, xs.dtype),
        input_output_aliases={3: 0},
        compiler_params=_params(("arbitrary",)),
        name="dispatch",
    )(pos1, pos2, x1t, xs)


def _expert_mlp_kernel(te_ref, tv_ref, tf_ref, x_ref, wg_ref, wu_ref, wd_ref, o_ref, wg_s, wu_s, wd_s):
    i = pl.program_id(0)

    @pl.when(tf_ref[i] == 1)
    def _():
        wg_s[...] = wg_ref[...].astype(BF16)
        wu_s[...] = wu_ref[...].astype(BF16)
        wd_s[...] = wd_ref[...].astype(BF16)

    @pl.when(tv_ref[i] == 1)
    def _():
        nc = wg_s.shape[0] // LANES
        x = _load_token_major(x_ref, MOE_TM, nc, BF16)
        g = jnp.dot(x, wg_s[...], preferred_element_type=F32)
        u = jnp.dot(x, wu_s[...], preferred_element_type=F32)
        hid = (jax.nn.silu(g) * u).astype(BF16)
        _store_token_major(o_ref, jnp.dot(hid, wd_s[...], preferred_element_type=F32))

    @pl.when(tv_ref[i] == 0)
    def _():
        o_ref[...] = jnp.zeros(o_ref.shape, F32)


def _expert_mlp(tile_expert, tile_valid, tile_first, xs, w_gate, w_up, w_down):
    d, ff = w_gate.shape[1], w_gate.shape[2]
    nc = d // LANES
    n_tiles = xs.shape[0] // (MOE_TM * nc)
    tok_spec = pl.BlockSpec((MOE_TM * nc, LANES), lambda i, te, tv, tf: (i, 0))
    grid_spec = pltpu.PrefetchScalarGridSpec(
        num_scalar_prefetch=3, grid=(n_tiles,),
        in_specs=[tok_spec,
                  pl.BlockSpec((None, d, ff), lambda i, te, tv, tf: (te[i], 0, 0)),
                  pl.BlockSpec((None, d, ff), lambda i, te, tv, tf: (te[i], 0, 0)),
                  pl.BlockSpec((None, ff, d), lambda i, te, tv, tf: (te[i], 0, 0))],
        out_specs=tok_spec,
        scratch_shapes=[pltpu.VMEM((d, ff), BF16), pltpu.VMEM((d, ff), BF16), pltpu.VMEM((ff, d), BF16)])
    return pl.pallas_call(
        _expert_mlp_kernel,
        grid_spec=grid_spec,
        out_shape=jax.ShapeDtypeStruct(xs.shape, F32),
        compiler_params=_params(("arbitrary",)),
        name="expert_mlp",
    )(tile_expert, tile_valid, tile_first, xs, w_gate, w_up, w_down)


def _combine_ln_kernel(pos1_ref, pos2_ref, x1t_ref, w1_ref, w2_ref, g_ref, b_ref, ys_hbm, o_ref, g1, g2, y_s, sem,
                       *, tm, nc):
    base = pl.program_id(0) * tm

    def row_copy(r, pos_ref, dst, slot):
        src = ys_hbm.at[pl.ds(pl.multiple_of(pos_ref[base + r] * nc, nc), nc)]
        return pltpu.make_async_copy(src, dst.at[pl.ds(pl.multiple_of(r * nc, nc), nc)], sem.at[slot])

    def start(r, _):
        row_copy(r, pos1_ref, g1, 0).start()
        row_copy(r, pos2_ref, g2, 1).start()
        return 0

    def wait(r, _):
        row_copy(r, pos1_ref, g1, 0).wait()
        row_copy(r, pos2_ref, g2, 1).wait()
        return 0

    lax.fori_loop(0, tm, start, 0)
    lax.fori_loop(0, tm, wait, 0)
    v = ALPHA * x1t_ref[...] + w1_ref[...] * g1[...] + w2_ref[...] * g2[...]
    v = v.reshape(tm, nc, LANES)
    tok_mean = lambda a: jnp.sum(jnp.sum(a, axis=2, keepdims=True), axis=1, keepdims=True) / (nc * LANES)
    xc = v - tok_mean(v)
    y = xc * lax.rsqrt(tok_mean(xc * xc) + LN_EPS) * g_ref[...][None] + b_ref[...][None]
    y_s[...] = y.reshape(tm * nc, LANES)
    for c in range(nc):
        o_ref[:, c * LANES:(c + 1) * LANES] = y_s[pl.ds(c, tm, stride=nc), :]


def _combine_ln(pos1, pos2, x1t, w1, w2, g, b, ys, nc):
    m = x1t.shape[0] // nc
    d = nc * LANES
    tm = min(POST_TM, m)
    tok = lambda w: pl.BlockSpec((tm * nc, w), lambda i, p1, p2: (i, 0))
    full = lambda a: pl.BlockSpec(a.shape, lambda i, p1, p2: (0, 0))
    grid_spec = pltpu.PrefetchScalarGridSpec(
        num_scalar_prefetch=2, grid=(m // tm,),
        in_specs=[tok(LANES), tok(1), tok(1), full(g), full(b), pl.BlockSpec(memory_space=pl.ANY)],
        out_specs=pl.BlockSpec((tm, d), lambda i, p1, p2: (i, 0)),
        scratch_shapes=[pltpu.VMEM((tm * nc, LANES), F32), pltpu.VMEM((tm * nc, LANES), F32),
                        pltpu.VMEM((tm * nc, LANES), F32), pltpu.SemaphoreType.DMA((2,))])
    return pl.pallas_call(
        functools.partial(_combine_ln_kernel, tm=tm, nc=nc),
        grid_spec=grid_spec,
        out_shape=jax.ShapeDtypeStruct((m, d), F32),
        compiler_params=_params(("arbitrary",)),
        name="combine_ln",
    )(pos1, pos2, x1t, w1, w2, g, b, ys)


def kernel(x_prompt, x_sample, cache_k_a, cache_v_a, cache_k_b, cache_v_b, cache_k_idx, page_table, w_in,
           lambda_q1, lambda_k1, lambda_q2, lambda_k2, subln_w, w_br_a, w_br_b, w_o, rel_bias, ln1_g, ln1_b,
           w_router_group, b_router_group, w_router_expert, b_router_expert, w_e_gate, w_e_up, w_e_down,
           ln2_g, ln2_b):
    assert w_in.shape[0] == DEPTH
    bp, t, d = x_prompt.shape
    assert bp == 1
    nb, ts, _ = x_sample.shape
    n_pages = page_table.shape[1]
    past_len = n_pages * PAGE_SIZE
    hd = H_A * 2 * DH_A
    n_phys = cache_k_a.shape[1]
    ms = nb * ts

    w = w_in[0]
    w_bf = w.astype(BF16)
    s0 = 7 * hd
    w_small = jnp.concatenate([w[:, s0:s0 + DH_I + H_I], jnp.zeros((d, LANES - DH_I - H_I), F32)], axis=1).astype(BF16)
    tab = rel_bias.T.astype(F32)
    lam4 = jnp.stack([lambda_q1[0], lambda_k1[0], lambda_q2[0], lambda_k2[0]]).astype(F32)
    subw = subln_w[0].reshape(1, 2 * DH_A).astype(F32)
    wa_bf = w_br_a[0].astype(BF16)
    wb_bf = w_br_b[0].astype(BF16)
    wo_bf = w_o[0].astype(BF16)
    w_route = jnp.concatenate([w_router_group[0], w_router_expert[0],
                               jnp.zeros((d, LANES - N_GROUPS - N_EXPERTS), F32)], axis=1)
    b_route = jnp.concatenate([b_router_group[0], b_router_expert[0],
                               jnp.zeros((LANES - N_GROUPS - N_EXPERTS,), F32)]).reshape(1, LANES)
    ln1g, ln1b = ln1_g[0].reshape(1, d), ln1_b[0].reshape(1, d)

    xp = x_prompt.reshape(t, d)
    xs = x_sample.reshape(ms, d)
    pp = _in_proj(xp.astype(BF16), w_bf, w_small)
    ps = _in_proj(xs.astype(BF16), w_bf, w_small)

    oa_p = _flash(pp["qa"], pp["ka"][1], pp["va"][1], None, tab, lam4, subw,
                  n_maps=2, scale=1.0, head_off=0, name="flash_diff")
    qh_p = pp["qi"].reshape(t, H_I, DH_I).transpose(1, 0, 2)
    mask_p = _index_select(qh_p, pp["small"], pp["ki_bf"], min(TOPK_MAX, t // 4))
    ob_p = _flash(pp["qb"], pp["kb"][1], pp["vb"][1], mask_p, tab, lam4, subw,
                  n_maps=1, scale=DH_B ** -0.5, head_off=H_A, name="flash_dsa")

    def rows_to_batch(a):
        if a.ndim == 4:
            return a.reshape(a.shape[0], 2, nb, ts, LANES).transpose(2, 0, 1, 3, 4).reshape(nb, a.shape[0], 2 * ts, LANES)
        a = a.reshape(a.shape[0], nb, ts, LANES).transpose(1, 0, 2, 3)
        return jnp.pad(a, ((0, 0), (0, 0), (0, SROWS - ts), (0, 0)))

    def new_rows(a):
        return jnp.pad(a.reshape(nb, ts, a.shape[1]), ((0, 0), (0, PAGE_SIZE - ts), (0, 0)))

    cka = cache_k_a[0].reshape(n_phys, PAGE_SIZE, hd)
    cva = cache_v_a[0].reshape(n_phys, PAGE_SIZE, hd)
    ckb = cache_k_b[0].reshape(n_phys, PAGE_SIZE, hd)
    cvb = cache_v_b[0].reshape(n_phys, PAGE_SIZE, hd)
    oa_s = _sample_attn(page_table, tab, lam4, subw, rows_to_batch(ps["qa"]), cka, cva,
                        new_rows(ps["ka"][0]), new_rows(ps["va"][0]), None,
                        n_maps=2, scale=1.0, head_off=0, name="sample_diff")
    qh_s = ps["qi"].reshape(nb, ts, H_I, DH_I).transpose(0, 2, 1, 3)
    qh_s = jnp.pad(qh_s, ((0, 0), (0, 0), (0, SROWS - ts), (0, 0))).reshape(nb, H_I * SROWS, DH_I)
    wi_s = ps["small"][:, DH_I:DH_I + H_I].reshape(nb, ts, H_I).transpose(0, 2, 1) * (DH_I ** -0.5 * H_I ** -0.5)
    wi_s = jnp.pad(wi_s, ((0, 0), (0, 0), (0, SROWS - ts))).reshape(nb, H_I * SROWS, 1)
    wb_s = jnp.broadcast_to(wi_s, (nb, H_I * SROWS, LANES))
    mask_s = _sample_select(page_table, qh_s, wb_s, cache_k_idx[0], new_rows(ps["ki"]),
                            min(TOPK_MAX, (past_len + ts) // 4))
    ob_s = _sample_attn(page_table, tab, lam4, subw, rows_to_batch(ps["qb"]), ckb, cvb,
                        new_rows(ps["kb"][0]), new_rows(ps["vb"][0]), mask_s,
                        n_maps=1, scale=DH_B ** -0.5, head_off=H_A, name="sample_dsa")
    oa_s = oa_s[:, :ts].reshape(ms, hd).astype(BF16)
    ob_s = ob_s[:, :ts].reshape(ms, hd).astype(BF16)

    mg_p = _gate_merge(oa_p, ob_p, pp["ga"], pp["gb"], wa_bf, wb_bf)
    mg_s = _gate_merge(oa_s, ob_s, ps["ga"], ps["gb"], wa_bf, wb_bf)
    x1_p, route_p, cnt_p = _out_ln_route(mg_p, xp, wo_bf, ln1g, ln1b, w_route, b_route, jnp.zeros((1, LANES), F32))
    x1_s, route_s, cnt = _out_ln_route(mg_s, xs, wo_bf, ln1g, ln1b, w_route, b_route, cnt_p)

    counts = cnt[0, :N_EXPERTS].astype(I32)
    padded = (counts + MOE_TM - 1) // MOE_TM * MOE_TM
    ends = jnp.cumsum(padded)
    starts = ends - padded
    n_tiles = (2 * (t + ms) + N_EXPERTS * (MOE_TM - 1) + MOE_TM - 1) // MOE_TM
    tile_row = jnp.arange(n_tiles, dtype=I32) * MOE_TM
    tile_expert = jnp.minimum(jnp.sum(tile_row[:, None] >= ends[None, :], axis=1), N_EXPERTS - 1).astype(I32)
    tile_valid = (tile_row < ends[-1]).astype(I32)
    tile_first = ((tile_row == starts[tile_expert]) & (tile_valid == 1)).astype(I32)

    def positions(route):
        e1 = route[:, 0].astype(I32)
        e2 = route[:, 1].astype(I32)
        return starts[e1] + route[:, 4].astype(I32), starts[e2] + route[:, 5].astype(I32)

    p1_p, p2_p = positions(route_p)
    p1_s, p2_s = positions(route_s)
    nc = d // LANES
    xsort = jnp.zeros((n_tiles * MOE_TM * nc, LANES), F32)
    xsort = _dispatch(p1_p, p2_p, x1_p, xsort, nc)
    xsort = _dispatch(p1_s, p2_s, x1_s, xsort, nc)
    ysort = _expert_mlp(tile_expert, tile_valid, tile_first, xsort, w_e_gate[0], w_e_up[0], w_e_down[0])
    per_row = lambda col: jnp.repeat(col, nc).reshape(-1, 1)
    ln2g_t, ln2b_t = ln2_g[0].reshape(nc, LANES), ln2_b[0].reshape(nc, LANES)
    y_p = _combine_ln(p1_p, p2_p, x1_p, per_row(route_p[:, 2]), per_row(route_p[:, 3]), ln2g_t, ln2b_t, ysort, nc)
    y_s = _combine_ln(p1_s, p2_s, x1_s, per_row(route_s[:, 2]), per_row(route_s[:, 3]), ln2g_t, ln2b_t, ysort, nc)

    kv5 = lambda a, n: a.reshape(1, n[0], n[1], H_A, 2 * DH_A)
    return (y_p.reshape(1, t, d), y_s.reshape(nb, ts, d),
            kv5(pp["ka"][0], (1, t)), kv5(pp["va"][0], (1, t)), kv5(pp["kb"][0], (1, t)), kv5(pp["vb"][0], (1, t)),
            pp["ki"].reshape(1, 1, t, DH_I),
            kv5(ps["ka"][0], (nb, ts)), kv5(ps["va"][0], (nb, ts)), kv5(ps["kb"][0], (nb, ts)),
            kv5(ps["vb"][0], (nb, ts)), ps["ki"].reshape(1, nb, ts, DH_I))
```

```python
import functools
import math

import numpy as np
import jax
import jax.numpy as jnp
from jax import lax
from jax.experimental import pallas as pl
from jax.experimental.pallas import tpu as pltpu

F32 = jnp.float32
BF16 = jnp.bfloat16
I32 = jnp.int32

H_A = 8
DH_A = 64
H_B = 8
DH_B = 128
H_I = 16
DH_I = 64
TOPK_MAX = 256
NUM_BUCKETS = 32
MAX_DISTANCE = 128
N_GROUPS = 4
EXPERTS_PER_GROUP = 8
N_EXPERTS = N_GROUPS * EXPERTS_PER_GROUP
PAGE_SIZE = 128
DEPTH = 1
ALPHA = (2 * DEPTH) ** 0.25
LN_EPS = 1e-5
LAMBDA_INIT = 0.8 - 0.6 * math.exp(-0.3 * 0)

LANES = 128
SUBLANES = 8
NEG = -1e30
INT_MIN = -2 ** 31
KEY_NEG_INF = int(np.array([-np.inf], np.float32).view(np.int32)[0]) ^ 0x7FFFFFFF
V7X_VMEM_LIMIT = 48 * 1024 * 1024

PROJ_TM = 512
FLASH_T = 512
SEL_TQ = 256
SEL_CH = 128
POST_TM = 256
MOE_TM = 256
SROWS = 8
N_NEW = 4
ATT_PAGES = 4
SEL_PAGES = 16


def _t5_thresholds():
    n = np.arange(0, MAX_DISTANCE + 1)
    max_exact = NUM_BUCKETS // 2
    nf = np.maximum(n, 1).astype(np.float32)
    large = max_exact + (np.log(nf / max_exact) / math.log(MAX_DISTANCE / max_exact)
                         * (NUM_BUCKETS - max_exact)).astype(np.int32)
    b = np.where(n < max_exact, n, np.minimum(large, NUM_BUCKETS - 1))
    assert np.all(np.diff(b) >= 0) and b[-1] == NUM_BUCKETS - 1
    return tuple(int(np.argmax(b >= j)) for j in range(1, NUM_BUCKETS))


T5_THRESH = _t5_thresholds()
T5_FAR = T5_THRESH[-1]


def _t5_bias(d, table):
    b = jnp.broadcast_to(table(0), d.shape).astype(F32)
    for j, t in enumerate(T5_THRESH, start=1):
        b = jnp.where(d >= t, table(j), b)
    return b


def _dot_nt(a, b):
    return lax.dot_general(a, b, (((1,), (1,)), ((), ())), preferred_element_type=F32)


def _sort_key(x):
    bits = lax.bitcast_convert_type(x, I32)
    return jnp.where(bits < 0, bits ^ 0x7FFFFFFF, bits)


def _params(sem):
    return pltpu.CompilerParams(dimension_semantics=sem, vmem_limit_bytes=V7X_VMEM_LIMIT)


def _proj_kernel(x_ref, w_ref, *rest, emit, n_extra):
    extra, out_refs = rest[:n_extra], rest[n_extra:]
    res = jnp.dot(x_ref[...], w_ref[...], preferred_element_type=F32)
    emit(res, out_refs, x_ref, *extra)


def _emit_f32(res, outs, x_ref):
    outs[0][...] = res


def _emit_bf16(res, outs, x_ref):
    outs[0][...] = res.astype(BF16)


def _emit_k(res, outs, x_ref):
    outs[0][...] = res
    for h in range(res.shape[1] // LANES):
        outs[1][h] = res[:, h * LANES:(h + 1) * LANES].astype(BF16)


def _emit_v(res, outs, x_ref, wt_ref):
    outs[0][...] = res
    res_t = _dot_nt(wt_ref[...], x_ref[...])
    for h in range(res.shape[1] // LANES):
        outs[1][h] = res_t[h * LANES:(h + 1) * LANES, :].astype(BF16)


def _emit_heads(res, outs, x_ref):
    for h in range(res.shape[1] // LANES):
        outs[0][h] = res[:, h * LANES:(h + 1) * LANES].astype(BF16)


def _emit_qa(res, outs, x_ref):
    lane = lax.broadcasted_iota(I32, (res.shape[0], LANES), 1)
    for h in range(res.shape[1] // LANES):
        blk = res[:, h * LANES:(h + 1) * LANES] * (DH_A ** -0.5)
        outs[0][h, 0] = jnp.where(lane < DH_A, blk, 0.0).astype(BF16)
        outs[0][h, 1] = jnp.where(lane >= DH_A, blk, 0.0).astype(BF16)


def _emit_small(res, outs, x_ref):
    outs[0][...] = res
    outs[1][...] = res[:, :DH_I]
    outs[2][...] = res[:, :DH_I].astype(BF16)


def _proj(x, w, emit, out_shapes, out_blocks, tok_axes, name, extra=()):
    m, k = x.shape
    tm = min(PROJ_TM, m)

    def spec(blk, tok_axis):
        nd = len(blk)
        return pl.BlockSpec(blk, lambda i: tuple(i if a == tok_axis else 0 for a in range(nd)))

    full = lambda a: pl.BlockSpec(a.shape, lambda i: (0,) * a.ndim)
    return pl.pallas_call(
        functools.partial(_proj_kernel, emit=emit, n_extra=len(extra)),
        grid=(m // tm,),
        in_specs=[pl.BlockSpec((tm, k), lambda i: (i, 0)), full(w)] + [full(e) for e in extra],
        out_specs=[spec(b, a) for b, a in zip(out_blocks, tok_axes)],
        out_shape=out_shapes,
        compiler_params=_params(("arbitrary",)),
        name=name,
    )(x, w, *extra)


def _in_proj(x, w_bf, w_small_bf, with_attention_layouts):
    m, d = x.shape
    tm = min(PROJ_TM, m)
    hd = H_A * 2 * DH_A
    sds = jax.ShapeDtypeStruct
    cols = lambda j: w_bf[:, j * hd:(j + 1) * hd]
    out = {}
    out["qa"], = _proj(x, cols(0), _emit_qa, [sds((H_A, 2, m, LANES), BF16)], [(H_A, 2, tm, LANES)], [2], "proj_qa")
    for j, nm in ((1, "ka"), (4, "kb")):
        if with_attention_layouts:
            out[nm], out[nm + "_h"] = _proj(x, cols(j), _emit_k, [sds((m, hd), F32), sds((H_A, m, LANES), BF16)],
                                            [(tm, hd), (H_A, tm, LANES)], [0, 1], "proj_" + nm)
        else:
            out[nm], = _proj(x, cols(j), _emit_f32, [sds((m, hd), F32)], [(tm, hd)], [0], "proj_" + nm)
    for j, nm in ((2, "va"), (5, "vb")):
        if with_attention_layouts:
            out[nm], out[nm + "_t"] = _proj(x, cols(j), _emit_v, [sds((m, hd), F32), sds((H_A, LANES, m), BF16)],
                                            [(tm, hd), (H_A, LANES, tm)], [0, 2], "proj_" + nm,
                                            extra=(cols(j).T,))
        else:
            out[nm], = _proj(x, cols(j), _emit_f32, [sds((m, hd), F32)], [(tm, hd)], [0], "proj_" + nm)
    out["qb"], = _proj(x, cols(3), _emit_heads, [sds((H_B, m, LANES), BF16)], [(H_B, tm, LANES)], [1], "proj_qb")
    out["qi"], = _proj(x, cols(6), _emit_bf16, [sds((m, hd), BF16)], [(tm, hd)], [0], "proj_qi")
    g0 = 7 * hd + DH_I + H_I
    out["ga"], = _proj(x, w_bf[:, g0:g0 + d], _emit_f32, [sds((m, d), F32)], [(tm, d)], [0], "proj_ga")
    out["gb"], = _proj(x, w_bf[:, g0 + d:g0 + 2 * d], _emit_f32, [sds((m, d), F32)], [(tm, d)], [0], "proj_gb")
    out["small"], out["ki"], out["ki_bf"] = _proj(
        x, w_small_bf, _emit_small, [sds((m, LANES), F32), sds((m, DH_I), F32), sds((m, DH_I), BF16)],
        [(tm, LANES), (tm, DH_I), (tm, DH_I)], [0, 0, 0], "proj_small")
    return out


def _lambda_full(lam_ref):
    a = jnp.sum(lam_ref[0:1, :] * lam_ref[1:2, :], axis=1, keepdims=True)
    b = jnp.sum(lam_ref[2:3, :] * lam_ref[3:4, :], axis=1, keepdims=True)
    return jnp.exp(a) - jnp.exp(b) + LAMBDA_INIT


def _flash_kernel(qs_ref, ks_ref, tab_ref, lam_ref, sub_ref, q_ref, k_ref, vt_ref, *rest,
                  n_maps, tq, use_mask, scale, head_off):
    if use_mask:
        m_ref, o_ref, acc, m_s, l_s, bias_s = rest
    else:
        o_ref, acc, m_s, l_s, bias_s = rest
    h = pl.program_id(0)
    step = pl.program_id(1)
    qi = qs_ref[step]
    ki = ks_ref[step]
    cols = n_maps * tq
    hb = h + head_off
    table = lambda j: tab_ref[hb, j]

    @pl.when(step == 0)
    def _():
        r = lax.broadcasted_iota(I32, (tq, tq), 0)
        c = lax.broadcasted_iota(I32, (tq, tq), 1)
        d0 = c - r
        near = jnp.where(d0 >= 0, _t5_bias(d0, table), NEG)
        far = _t5_bias(d0 + tq, table)
        for mp in range(n_maps):
            bias_s[0, :, mp * tq:(mp + 1) * tq] = near
            bias_s[1, :, mp * tq:(mp + 1) * tq] = far

    @pl.when(ki == 0)
    def _():
        m_s[...] = jnp.full(m_s.shape, NEG, F32)
        l_s[...] = jnp.zeros(l_s.shape, F32)
        acc[...] = jnp.zeros(acc.shape, F32)

    def update(bias):
        q = q_ref[...].reshape(cols, LANES)
        s = _dot_nt(k_ref[...], q)
        if scale != 1.0:
            s = s * scale
        s = s + bias
        if use_mask:
            s = s + m_ref[...]
        m_old = m_s[...]
        m_new = jnp.maximum(m_old, jnp.max(s, axis=0, keepdims=True))
        alpha = jnp.exp(m_old - m_new)
        p = jnp.exp(s - m_new)
        l_s[...] = alpha * l_s[...] + jnp.sum(p, axis=0, keepdims=True)
        acc[...] = alpha * acc[...] + jnp.dot(vt_ref[...], p.astype(BF16), preferred_element_type=F32)
        m_s[...] = m_new

    @pl.when(qi - ki >= 2)
    def _():
        update(tab_ref[hb, NUM_BUCKETS - 1])

    @pl.when(qi - ki < 2)
    def _():
        update(bias_s[qi - ki])

    @pl.when(ki == qi)
    def _():
        o = acc[...] / l_s[...]
        if n_maps == 2:
            o = o[:, :tq] - _lambda_full(lam_ref) * o[:, tq:]
            o = o * lax.rsqrt(jnp.mean(o * o, axis=0, keepdims=True) + LN_EPS) * sub_ref[...] * (1.0 - LAMBDA_INIT)
        o_ref[...] = o.T.astype(BF16)


def _flash(q, k, vt, mask_t, tab, lam4, subw_col, *, n_maps, scale, head_off, name):
    nh, t = k.shape[0], k.shape[1]
    tq = min(FLASH_T, t)
    nq = t // tq
    pairs = [(a, b) for a in range(nq) for b in range(a + 1)]
    qs = jnp.asarray([p[0] for p in pairs], I32)
    ks = jnp.asarray([p[1] for p in pairs], I32)
    cols = n_maps * tq
    smem = pl.BlockSpec(memory_space=pltpu.SMEM)
    if n_maps == 2:
        q_spec = pl.BlockSpec((None, 2, tq, LANES), lambda h, s, qs, ks: (h, 0, qs[s], 0))
    else:
        q_spec = pl.BlockSpec((None, tq, LANES), lambda h, s, qs, ks: (h, qs[s], 0))
    in_specs = [smem, pl.BlockSpec((4, DH_A), lambda h, s, qs, ks: (0, 0)),
                pl.BlockSpec((LANES, 1), lambda h, s, qs, ks: (0, 0)), q_spec,
                pl.BlockSpec((None, tq, LANES), lambda h, s, qs, ks: (h, ks[s], 0)),
                pl.BlockSpec((None, LANES, tq), lambda h, s, qs, ks: (h, 0, ks[s]))]
    args = [tab, lam4, subw_col, q, k, vt]
    if mask_t is not None:
        in_specs.append(pl.BlockSpec((tq, tq), lambda h, s, qs, ks: (ks[s], qs[s])))
        args.append(mask_t)
    grid_spec = pltpu.PrefetchScalarGridSpec(
        num_scalar_prefetch=2,
        grid=(nh, len(pairs)),
        in_specs=in_specs,
        out_specs=pl.BlockSpec((tq, LANES), lambda h, s, qs, ks: (qs[s], h)),
        scratch_shapes=[pltpu.VMEM((LANES, cols), F32), pltpu.VMEM((1, cols), F32), pltpu.VMEM((1, cols), F32),
                        pltpu.VMEM((2, tq, cols), F32)],
    )
    return pl.pallas_call(
        functools.partial(_flash_kernel, n_maps=n_maps, tq=tq, use_mask=mask_t is not None, scale=scale,
                          head_off=head_off),
        grid_spec=grid_spec,
        out_shape=jax.ShapeDtypeStruct((t, nh * LANES), BF16),
        compiler_params=_params(("arbitrary", "arbitrary")),
        name=name,
    )(qs, ks, *args)


def _kth_largest_key(count_ge, shape, k_top):
    cand0 = jnp.zeros(shape, I32)
    res = jnp.where(count_ge(cand0) >= k_top, cand0, jnp.full(shape, INT_MIN, I32))

    def bit_body(b, res):
        cand = res + jnp.left_shift(jnp.int32(1), jnp.int32(30) - b)
        return jnp.where(count_ge(cand) >= k_top, cand, res)

    return lax.fori_loop(0, 31, bit_body, res)


def _index_select_kernel(qh_ref, wt_ref, ki_ref, o_ref, keys_s, *, tq, ch, n_ch, k_top):
    i = pl.program_id(0)
    q0 = i * tq
    n_valid = (q0 + tq + ch - 1) // ch
    wt = wt_ref[...] * (DH_I ** -0.5 * H_I ** -0.5)
    krow = lax.broadcasted_iota(I32, (ch, tq), 0)
    qpos = lax.broadcasted_iota(I32, (ch, tq), 1) + q0

    def score_chunk(c, _):
        k0 = pl.multiple_of(c * ch, ch)
        kc = ki_ref[pl.ds(k0, ch), :]
        acc = jnp.zeros((ch, tq), F32)
        for h in range(H_I):
            acc = acc + wt[h:h + 1, :] * jnp.maximum(_dot_nt(kc, qh_ref[h]), 0.0)
        val = jnp.where(krow + k0 <= qpos, acc, -jnp.inf)
        keys_s[pl.ds(k0, ch), :] = _sort_key(val)
        return 0

    lax.fori_loop(0, n_valid, score_chunk, 0)

    def count_ge(cand):
        def chunk(c, a):
            blk = keys_s[pl.ds(pl.multiple_of(c * ch, ch), ch), :]
            hit = (blk >= cand[0:1, :]).astype(I32)
            return a + jnp.sum(hit.reshape(ch // SUBLANES, SUBLANES, tq), axis=0)
        a = lax.fori_loop(0, n_valid, chunk, jnp.zeros((SUBLANES, tq), I32))
        return jnp.sum(a.astype(F32), axis=0, keepdims=True)

    thr = _kth_largest_key(count_ge, (SUBLANES, tq), k_top)[0:1, :]

    def write_chunk(c, _):
        k0 = pl.multiple_of(c * ch, ch)
        key = keys_s[pl.ds(k0, ch), :]
        o_ref[pl.ds(k0, ch), :] = jnp.where((key >= thr) & (key > KEY_NEG_INF), 0.0, NEG)
        return 0

    def fill_chunk(c, _):
        o_ref[pl.ds(pl.multiple_of(c * ch, ch), ch), :] = jnp.full((ch, tq), NEG, F32)
        return 0

    lax.fori_loop(0, n_valid, write_chunk, 0)
    lax.fori_loop(n_valid, n_ch, fill_chunk, 0)


def _index_select(qh, wt, ki_bf, k_top):
    t = ki_bf.shape[0]
    tq = min(SEL_TQ, t)
    ch = min(SEL_CH, t)
    return pl.pallas_call(
        functools.partial(_index_select_kernel, tq=tq, ch=ch, n_ch=t // ch, k_top=k_top),
        grid=(t // tq,),
        in_specs=[pl.BlockSpec((H_I, tq, DH_I), lambda i: (0, i, 0)),
                  pl.BlockSpec((H_I, tq), lambda i: (0, i)),
                  pl.BlockSpec((t, DH_I), lambda i: (0, 0))],
        out_specs=pl.BlockSpec((t, tq), lambda i: (0, i)),
        out_shape=jax.ShapeDtypeStruct((t, t), F32),
        scratch_shapes=[pltpu.VMEM((t, tq), I32)],
        compiler_params=_params(("arbitrary",)),
        name="index_select",
    )(qh, wt, ki_bf)


def _sample_select_kernel(pt_ref, q_ref, wb_ref, *rest, n_pages, n_out, k_top):
    kc_refs, (kn_ref, o_ref, keys_s) = rest[:SEL_PAGES], rest[SEL_PAGES:]
    j = pl.program_id(1)
    n_steps = n_pages // SEL_PAGES
    row = lax.broadcasted_iota(I32, (SROWS, LANES), 0)
    lane = lax.broadcasted_iota(I32, (SROWS, LANES), 1)
    tok = row % N_NEW

    def score(k_f32):
        s = jnp.maximum(_dot_nt(q_ref[...], k_f32.astype(BF16)), 0.0) * wb_ref[...]
        return jnp.sum(s.reshape(H_I, SROWS, LANES), axis=0)

    @pl.when(j < n_steps)
    def _():
        for g in range(SEL_PAGES):
            keys_s[j * SEL_PAGES + g] = _sort_key(score(kc_refs[g][...]))

    @pl.when(j == n_steps)
    def _():
        val = jnp.where((lane <= tok) & (lane < N_NEW), score(kn_ref[...]), -jnp.inf)
        keys_s[n_pages] = _sort_key(val)

        def count_ge(cand):
            a = lax.fori_loop(0, n_pages + 1, lambda c, a: a + (keys_s[c] >= cand).astype(I32),
                              jnp.zeros((SROWS, LANES), I32))
            return jnp.sum(a.astype(F32), axis=1, keepdims=True)

        thr = _kth_largest_key(count_ge, (SROWS, LANES), k_top)

        def write(c, _):
            key = keys_s[c]
            o_ref[c] = jnp.where((key >= thr) & (key > KEY_NEG_INF), 0.0, NEG)
            return 0

        lax.fori_loop(0, n_pages + 1, write, 0)
        for c in range(n_pages + 1, n_out):
            o_ref[c] = jnp.full((SROWS, LANES), NEG, F32)


def _sample_select(page_table, qh, wb, cache_k_idx, ki_new, k_top, n_out):
    nb, n_pages = page_table.shape
    assert n_pages % SEL_PAGES == 0

    def page_spec(g):
        return pl.BlockSpec((None, PAGE_SIZE, DH_I),
                            lambda b, j, pt: (pt[b, jnp.minimum(j * SEL_PAGES + g, n_pages - 1)], 0, 0))

    grid_spec = pltpu.PrefetchScalarGridSpec(
        num_scalar_prefetch=1,
        grid=(nb, n_pages // SEL_PAGES + 1),
        in_specs=[pl.BlockSpec((None, H_I * SROWS, DH_I), lambda b, j, pt: (b, 0, 0)),
                  pl.BlockSpec((None, H_I * SROWS, LANES), lambda b, j, pt: (b, 0, 0))]
                 + [page_spec(g) for g in range(SEL_PAGES)]
                 + [pl.BlockSpec((None, PAGE_SIZE, DH_I), lambda b, j, pt: (b, 0, 0))],
        out_specs=pl.BlockSpec((None, n_out, SROWS, LANES), lambda b, j, pt: (b, 0, 0, 0)),
        scratch_shapes=[pltpu.VMEM((n_pages + 1, SROWS, LANES), I32)],
    )
    return pl.pallas_call(
        functools.partial(_sample_select_kernel, n_pages=n_pages, n_out=n_out, k_top=k_top),
        grid_spec=grid_spec,
        out_shape=jax.ShapeDtypeStruct((nb, n_out, SROWS, LANES), F32),
        compiler_params=_params(("arbitrary", "arbitrary")),
        name="sample_select",
    )(page_table, qh, wb, *([cache_k_idx] * SEL_PAGES), ki_new)


def _sample_attn_kernel(pt_ref, tabr_ref, lam_ref, sub_ref, q_ref, *rest,
                        n_maps, use_mask, scale, n_pages, n_heads):
    g_n = ATT_PAGES
    kc_refs, vc_refs = rest[:g_n], rest[g_n:2 * g_n]
    rest = rest[2 * g_n:]
    if use_mask:
        kn_ref, vn_ref, m_ref, o_ref, acc, m_s, l_s = rest
    else:
        kn_ref, vn_ref, o_ref, acc, m_s, l_s = rest
    j = pl.program_id(1)
    n_steps = n_pages // g_n
    past_len = n_pages * PAGE_SIZE
    rows = n_heads * SROWS
    row = lax.broadcasted_iota(I32, (rows, LANES), 0)
    lane = lax.broadcasted_iota(I32, (rows, LANES), 1)
    tok = row % N_NEW
    table = lambda b: tabr_ref[:, b:b + 1]

    @pl.when(j == 0)
    def _():
        m_s[...] = jnp.full(m_s.shape, NEG, F32)
        l_s[...] = jnp.zeros(l_s.shape, F32)
        acc[...] = jnp.zeros(acc.shape, F32)

    def update(k_ref, v_ref, bias, page_mask):
        s = jnp.concatenate(
            [_dot_nt(q_ref[h], k_ref[pl.ds(h, PAGE_SIZE, stride=n_heads), :].astype(BF16)) for h in range(n_heads)],
            axis=0)
        if scale != 1.0:
            s = s * scale
        s = s + bias
        if page_mask is not None:
            s = s + jnp.concatenate([page_mask] * n_heads, axis=0)
        m_old = m_s[...]
        m_new = jnp.maximum(m_old, jnp.max(s, axis=1, keepdims=True))
        alpha = jnp.exp(m_old - m_new)
        pr = jnp.exp(s - m_new)
        l_s[...] = alpha * l_s[...] + jnp.sum(pr, axis=1, keepdims=True)
        pv = jnp.concatenate(
            [jnp.dot(pr[h * SROWS:(h + 1) * SROWS].astype(BF16),
                     v_ref[pl.ds(h, PAGE_SIZE, stride=n_heads), :].astype(BF16), preferred_element_type=F32)
             for h in range(n_heads)], axis=0)
        acc[...] = alpha * acc[...] + pv
        m_s[...] = m_new

    def page_bias(page, exact):
        if not exact:
            return jnp.broadcast_to(table(NUM_BUCKETS - 1), (rows, LANES))
        return _t5_bias((past_len + tok) - (page * PAGE_SIZE + lane), table)

    far_steps = (past_len - T5_FAR - PAGE_SIZE + 1) // (PAGE_SIZE * g_n)

    def pages(exact):
        for g in range(g_n):
            page = j * g_n + g
            update(kc_refs[g], vc_refs[g], page_bias(page, exact), m_ref[g] if use_mask else None)

    @pl.when(j < far_steps)
    def _():
        pages(False)

    @pl.when((j >= far_steps) & (j < n_steps))
    def _():
        pages(True)

    @pl.when(j == n_steps)
    def _():
        dist = tok - lane
        bias = jnp.where((dist >= 0) & (lane < N_NEW), _t5_bias(dist, table), NEG)
        update(kn_ref, vn_ref, bias, m_ref[0] if use_mask else None)
        o = acc[...] / l_s[...]
        for h in range(n_heads):
            oh = o[h * SROWS:(h + 1) * SROWS]
            if n_maps == 2:
                oh = oh - _lambda_full(lam_ref) * pltpu.roll(oh, N_NEW, 0)
                oh = oh * lax.rsqrt(jnp.mean(oh * oh, axis=1, keepdims=True) + LN_EPS) * sub_ref[...] \
                    * (1.0 - LAMBDA_INIT)
            o_ref[:, h * LANES:(h + 1) * LANES] = oh


def _sample_attn(page_table, tab_rows, lam4, subw, q, cache_k, cache_v, k_new, v_new, mask, *, n_maps, scale, name):
    nb, n_pages = page_table.shape
    assert n_pages % ATT_PAGES == 0
    nh = q.shape[1]
    slab = PAGE_SIZE * nh

    def page_spec(g):
        return pl.BlockSpec((None, slab, LANES),
                            lambda b, j, pt: (pt[b, jnp.minimum(j * ATT_PAGES + g, n_pages - 1)], 0, 0))

    new_spec = pl.BlockSpec((None, slab, LANES), lambda b, j, pt: (b, 0, 0))
    in_specs = [pl.BlockSpec(tab_rows.shape, lambda b, j, pt: (0, 0)),
                pl.BlockSpec((4, DH_A), lambda b, j, pt: (0, 0)),
                pl.BlockSpec((1, LANES), lambda b, j, pt: (0, 0)),
                pl.BlockSpec((None, nh, SROWS, LANES), lambda b, j, pt: (b, 0, 0, 0))]
    in_specs += [page_spec(g) for g in range(ATT_PAGES)] * 2 + [new_spec, new_spec]
    args = [tab_rows, lam4, subw, q] + [cache_k] * ATT_PAGES + [cache_v] * ATT_PAGES + [k_new, v_new]
    if mask is not None:
        in_specs.append(pl.BlockSpec((None, ATT_PAGES, SROWS, LANES), lambda b, j, pt: (b, j, 0, 0)))
        args.append(mask)
    grid_spec = pltpu.PrefetchScalarGridSpec(
        num_scalar_prefetch=1,
        grid=(nb, n_pages // ATT_PAGES + 1),
        in_specs=in_specs,
        out_specs=pl.BlockSpec((None, SROWS, nh * LANES), lambda b, j, pt: (b, 0, 0)),
        scratch_shapes=[pltpu.VMEM((nh * SROWS, LANES), F32), pltpu.VMEM((nh * SROWS, 1), F32),
                        pltpu.VMEM((nh * SROWS, 1), F32)],
    )
    return pl.pallas_call(
        functools.partial(_sample_attn_kernel, n_maps=n_maps, use_mask=mask is not None, scale=scale,
                          n_pages=n_pages, n_heads=nh),
        grid_spec=grid_spec,
        out_shape=jax.ShapeDtypeStruct((nb, SROWS, nh * LANES), F32),
        compiler_params=_params(("arbitrary", "arbitrary")),
        name=name,
    )(page_table, *args)


def _gate_merge_kernel(oa_ref, ob_ref, ga_ref, gb_ref, wa_ref, wb_ref, o_ref):
    a = jnp.dot(oa_ref[...], wa_ref[...], preferred_element_type=F32)
    b = jnp.dot(ob_ref[...], wb_ref[...], preferred_element_type=F32)
    o_ref[...] = (jax.nn.sigmoid(ga_ref[...]) * a + jax.nn.sigmoid(gb_ref[...]) * b).astype(BF16)


def _gate_merge(oa, ob, ga, gb, wa, wb):
    m, d = ga.shape
    tm = min(POST_TM, m)
    row = lambda w: pl.BlockSpec((tm, w), lambda i: (i, 0))
    full = lambda a: pl.BlockSpec(a.shape, lambda i: (0, 0))
    return pl.pallas_call(
        _gate_merge_kernel,
        grid=(m // tm,),
        in_specs=[row(oa.shape[1]), row(ob.shape[1]), row(d), row(d), full(wa), full(wb)],
        out_specs=row(d),
        out_shape=jax.ShapeDtypeStruct((m, d), BF16),
        compiler_params=_params(("arbitrary",)),
        name="gate_merge",
    )(oa, ob, ga, gb, wa, wb)


def _store_token_major(ref, x):
    tm, d = x.shape
    nc = d // LANES
    for c in range(nc):
        ref[pl.ds(c, tm, stride=nc), :] = x[:, c * LANES:(c + 1) * LANES]


def _load_token_major(ref, start, tm, nc, dtype):
    return jnp.concatenate([ref[pl.ds(start + c, tm, stride=nc), :].astype(dtype) for c in range(nc)], axis=1)


def _layer_norm(x, g_ref, b_ref):
    mu = jnp.mean(x, axis=1, keepdims=True)
    xc = x - mu
    var = jnp.mean(xc * xc, axis=1, keepdims=True)
    return xc * lax.rsqrt(var + LN_EPS) * g_ref[...] + b_ref[...]


def _out_ln_route_kernel(mg_ref, x_ref, wo_ref, g_ref, b_ref, wr_ref, br_ref, cin_ref, *rest, tm):
    x1t_ref, route_ref, cnt_ref, carry = rest[-4:]
    i = pl.program_id(0)

    @pl.when(i == 0)
    def _():
        carry[...] = cin_ref[...]

    mix = jnp.dot(mg_ref[...], wo_ref[...], preferred_element_type=F32)
    x1 = _layer_norm(ALPHA * x_ref[...] + mix, g_ref, b_ref)
    _store_token_major(x1t_ref, x1)

    z = jnp.dot(x1, wr_ref[...], preferred_element_type=F32, precision=lax.Precision.HIGHEST) + br_ref[...]
    lane = lax.broadcasted_iota(I32, (tm, LANES), 1).astype(F32)
    ninf = -jnp.inf
    big = jnp.float32(2 ** 30)
    gl = jnp.where(lane < N_GROUPS, z, ninf)
    gmax = jnp.max(gl, axis=1, keepdims=True)
    gsel = jnp.min(jnp.where(gl == gmax, lane, big), axis=1, keepdims=True)
    p_g = 1.0 / jnp.sum(jnp.exp(gl - gmax), axis=1, keepdims=True)
    e_lane = lane - N_GROUPS
    in_grp = (e_lane >= gsel * EXPERTS_PER_GROUP) & (e_lane < (gsel + 1) * EXPERTS_PER_GROUP)
    el = jnp.where(in_grp, z, ninf)
    v1 = jnp.max(el, axis=1, keepdims=True)
    i1 = jnp.min(jnp.where(el == v1, lane, big), axis=1, keepdims=True)
    el2 = jnp.where(lane == i1, ninf, el)
    v2 = jnp.max(el2, axis=1, keepdims=True)
    i2 = jnp.min(jnp.where(el2 == v2, lane, big), axis=1, keepdims=True)
    e2x = jnp.exp(v2 - v1)
    w1 = p_g / (1.0 + e2x)
    w2 = p_g * e2x / (1.0 + e2x)
    e1 = i1 - N_GROUPS
    e2 = i2 - N_GROUPS

    hot1 = lane == e1
    hot2 = lane == e2
    onehot = (hot1 | hot2).astype(BF16)
    r = lax.broadcasted_iota(I32, (tm, tm), 0)
    c = lax.broadcasted_iota(I32, (tm, tm), 1)
    tri = (r > c).astype(BF16)
    prefix = jnp.dot(tri, onehot, preferred_element_type=F32) + carry[...]
    rank1 = jnp.sum(jnp.where(hot1, prefix, 0.0), axis=1, keepdims=True)
    rank2 = jnp.sum(jnp.where(hot2, prefix, 0.0), axis=1, keepdims=True)
    carry[...] = carry[...] + jnp.sum(onehot.astype(F32), axis=0, keepdims=True)
    cnt_ref[...] = carry[...]

    route = jnp.zeros((tm, LANES), F32)
    for j, col in enumerate((e1, e2, w1, w2, rank1, rank2)):
        route = jnp.where(lane == j, col, route)
    route_ref[...] = route


def _out_ln_route(merged, x, wo, g, b, wr, br, carry_in, x1t_prev, tok_off, n_tok_total):
    m, d = x.shape
    tm = min(POST_TM, m)
    nc = d // LANES
    assert tok_off % tm == 0
    blk_off = tok_off // tm
    row = lambda w: pl.BlockSpec((tm, w), lambda i: (i, 0))
    full = lambda a: pl.BlockSpec(a.shape, lambda i: (0, 0))
    sds = jax.ShapeDtypeStruct
    in_specs = [row(d), row(d), full(wo), full(g), full(b), full(wr), full(br), full(carry_in)]
    args = [merged, x, wo, g, b, wr, br, carry_in]
    aliases = {}
    if x1t_prev is not None:
        in_specs.append(pl.BlockSpec(memory_space=pl.ANY))
        args.append(x1t_prev)
        aliases = {len(args) - 1: 0}
    return pl.pallas_call(
        functools.partial(_out_ln_route_kernel, tm=tm),
        grid=(m // tm,),
        in_specs=in_specs,
        out_specs=[pl.BlockSpec((tm * nc, LANES), lambda i: (i + blk_off, 0)), row(LANES),
                   pl.BlockSpec((1, LANES), lambda i: (0, 0))],
        out_shape=[sds((n_tok_total * nc, LANES), F32), sds((m, LANES), F32), sds((1, LANES), F32)],
        scratch_shapes=[pltpu.VMEM((1, LANES), F32)],
        input_output_aliases=aliases,
        compiler_params=_params(("arbitrary",)),
        name="out_ln_route",
    )(*args)


def _inverse_perm_kernel(pos1_ref, pos2_ref, inv_ref, *, n_tok, n_slots):
    def zero(s, _):
        inv_ref[s] = 0
        return 0

    def put(t, _):
        inv_ref[pos1_ref[t]] = t
        inv_ref[pos2_ref[t]] = t
        return 0

    lax.fori_loop(0, n_slots, zero, 0)
    lax.fori_loop(0, n_tok, put, 0)


def _inverse_perm(pos1, pos2, n_slots):
    smem = pl.BlockSpec(memory_space=pltpu.SMEM)
    return pl.pallas_call(
        functools.partial(_inverse_perm_kernel, n_tok=pos1.shape[0], n_slots=n_slots),
        in_specs=[smem, smem],
        out_specs=smem,
        out_shape=jax.ShapeDtypeStruct((n_slots,), I32),
        name="inverse_perm",
    )(pos1, pos2)


def _expert_mlp_kernel(te_ref, tv_ref, tf_ref, inv_ref, x_hbm, wg_ref, wu_ref, wd_ref, o_ref,
                       wg_s, wu_s, wd_s, xbuf, sem, *, n_tiles, nc):
    i = pl.program_id(0)
    slab = MOE_TM * nc

    def row_copy(tile, r):
        slot = tile % 2
        src = x_hbm.at[pl.ds(pl.multiple_of(inv_ref[tile * MOE_TM + r] * nc, nc), nc)]
        dst = xbuf.at[pl.ds(pl.multiple_of(slot * slab + r * nc, nc), nc)]
        return pltpu.make_async_copy(src, dst, sem.at[slot])

    def start_gather(tile):
        lax.fori_loop(0, MOE_TM, lambda r, _: (row_copy(tile, r).start(), 0)[1], 0)

    def wait_gather(tile):
        lax.fori_loop(0, MOE_TM, lambda r, _: (row_copy(tile, r).wait(), 0)[1], 0)

    @pl.when((i == 0) & (tv_ref[0] == 1))
    def _():
        start_gather(0)

    @pl.when((i + 1 < n_tiles) & (tv_ref[jnp.minimum(i + 1, n_tiles - 1)] == 1))
    def _():
        start_gather(i + 1)

    @pl.when(tf_ref[i] == 1)
    def _():
        wg_s[...] = wg_ref[...].astype(BF16)
        wu_s[...] = wu_ref[...].astype(BF16)
        wd_s[...] = wd_ref[...].astype(BF16)

    @pl.when(tv_ref[i] == 1)
    def _():
        wait_gather(i)
        x = _load_token_major(xbuf, (i % 2) * slab, MOE_TM, nc, BF16)
        g = jnp.dot(x, wg_s[...], preferred_element_type=F32)
        u = jnp.dot(x, wu_s[...], preferred_element_type=F32)
        hid = (jax.nn.silu(g) * u).astype(BF16)
        _store_token_major(o_ref, jnp.dot(hid, wd_s[...], preferred_element_type=F32))

    @pl.when(tv_ref[i] == 0)
    def _():
        o_ref[...] = jnp.zeros(o_ref.shape, F32)


def _expert_mlp(tile_expert, tile_valid, tile_first, inv, x1t, w_gate, w_up, w_down):
    d, ff = w_gate.shape[1], w_gate.shape[2]
    nc = d // LANES
    n_tiles = inv.shape[0] // MOE_TM
    grid_spec = pltpu.PrefetchScalarGridSpec(
        num_scalar_prefetch=4, grid=(n_tiles,),
        in_specs=[pl.BlockSpec(memory_space=pl.ANY),
                  pl.BlockSpec((None, d, ff), lambda i, te, tv, tf, inv: (te[i], 0, 0)),
                  pl.BlockSpec((None, d, ff), lambda i, te, tv, tf, inv: (te[i], 0, 0)),
                  pl.BlockSpec((None, ff, d), lambda i, te, tv, tf, inv: (te[i], 0, 0))],
        out_specs=pl.BlockSpec((MOE_TM * nc, LANES), lambda i, te, tv, tf, inv: (i, 0)),
        scratch_shapes=[pltpu.VMEM((d, ff), BF16), pltpu.VMEM((d, ff), BF16), pltpu.VMEM((ff, d), BF16),
                        pltpu.VMEM((2 * MOE_TM * nc, LANES), F32), pltpu.SemaphoreType.DMA((2,))])
    return pl.pallas_call(
        functools.partial(_expert_mlp_kernel, n_tiles=n_tiles, nc=nc),
        grid_spec=grid_spec,
        out_shape=jax.ShapeDtypeStruct((inv.shape[0] * nc, LANES), F32),
        compiler_params=_params(("arbitrary",)),
        name="expert_mlp",
    )(tile_expert, tile_valid, tile_first, inv, x1t, w_gate, w_up, w_down)


def _combine_ln_kernel(pos1_ref, pos2_ref, x1t_ref, w1_ref, w2_ref, g_ref, b_ref, ys_hbm, o_ref, g1, g2, y_s, sem,
                       *, tm, nc, tok_off):
    base = tok_off + pl.program_id(0) * tm

    def row_copy(r, pos_ref, dst, slot):
        src = ys_hbm.at[pl.ds(pl.multiple_of(pos_ref[base + r] * nc, nc), nc)]
        return pltpu.make_async_copy(src, dst.at[pl.ds(pl.multiple_of(r * nc, nc), nc)], sem.at[slot])

    def start(r, _):
        row_copy(r, pos1_ref, g1, 0).start()
        row_copy(r, pos2_ref, g2, 1).start()
        return 0

    def wait(r, _):
        row_copy(r, pos1_ref, g1, 0).wait()
        row_copy(r, pos2_ref, g2, 1).wait()
        return 0

    lax.fori_loop(0, tm, start, 0)
    lax.fori_loop(0, tm, wait, 0)
    v = ALPHA * x1t_ref[...] + w1_ref[...] * g1[...] + w2_ref[...] * g2[...]
    v = v.reshape(tm, nc, LANES)
    tok_mean = lambda a: jnp.sum(jnp.sum(a, axis=2, keepdims=True), axis=1, keepdims=True) / (nc * LANES)
    xc = v - tok_mean(v)
    y = xc * lax.rsqrt(tok_mean(xc * xc) + LN_EPS) * g_ref[...][None] + b_ref[...][None]
    y_s[...] = y.reshape(tm * nc, LANES)
    for c in range(nc):
        o_ref[:, c * LANES:(c + 1) * LANES] = y_s[pl.ds(c, tm, stride=nc), :]


def _combine_ln(pos1, pos2, x1t, w1, w2, g, b, ys, nc, tok_off, m):
    d = nc * LANES
    tm = min(POST_TM, m)
    assert tok_off % tm == 0
    blk_off = tok_off // tm
    tok = lambda w: pl.BlockSpec((tm * nc, w), lambda i, p1, p2: (i + blk_off, 0))
    full = lambda a: pl.BlockSpec(a.shape, lambda i, p1, p2: (0, 0))
    grid_spec = pltpu.PrefetchScalarGridSpec(
        num_scalar_prefetch=2, grid=(m // tm,),
        in_specs=[tok(LANES), tok(1), tok(1), full(g), full(b), pl.BlockSpec(memory_space=pl.ANY)],
        out_specs=pl.BlockSpec((tm, d), lambda i, p1, p2: (i, 0)),
        scratch_shapes=[pltpu.VMEM((tm * nc, LANES), F32), pltpu.VMEM((tm * nc, LANES), F32),
                        pltpu.VMEM((tm * nc, LANES), F32), pltpu.SemaphoreType.DMA((2,))])
    return pl.pallas_call(
        functools.partial(_combine_ln_kernel, tm=tm, nc=nc, tok_off=tok_off),
        grid_spec=grid_spec,
        out_shape=jax.ShapeDtypeStruct((m, d), F32),
        compiler_params=_params(("arbitrary",)),
        name="combine_ln",
    )(pos1, pos2, x1t, w1, w2, g, b, ys)


def kernel(x_prompt, x_sample, cache_k_a, cache_v_a, cache_k_b, cache_v_b, cache_k_idx, page_table, w_in,
           lambda_q1, lambda_k1, lambda_q2, lambda_k2, subln_w, w_br_a, w_br_b, w_o, rel_bias, ln1_g, ln1_b,
           w_router_group, b_router_group, w_router_expert, b_router_expert, w_e_gate, w_e_up, w_e_down,
           ln2_g, ln2_b):
    assert w_in.shape[0] == DEPTH
    bp, t, d = x_prompt.shape
    assert bp == 1
    nb, ts, _ = x_sample.shape
    assert ts == N_NEW
    n_pages = page_table.shape[1]
    past_len = n_pages * PAGE_SIZE
    hd = H_A * 2 * DH_A
    n_phys = cache_k_a.shape[1]
    ms = nb * ts
    nc = d // LANES

    w = w_in[0]
    w_bf = w.astype(BF16)
    s0 = 7 * hd
    w_small = jnp.concatenate([w[:, s0:s0 + DH_I + H_I], jnp.zeros((d, LANES - DH_I - H_I), F32)], axis=1).astype(BF16)
    tab = rel_bias.T.astype(F32)
    lam4 = jnp.stack([lambda_q1[0], lambda_k1[0], lambda_q2[0], lambda_k2[0]]).astype(F32)
    subw = subln_w[0].reshape(1, 2 * DH_A).astype(F32)
    wa_bf = w_br_a[0].astype(BF16)
    wb_bf = w_br_b[0].astype(BF16)
    wo_bf = w_o[0].astype(BF16)
    w_route = jnp.concatenate([w_router_group[0], w_router_expert[0],
                               jnp.zeros((d, LANES - N_GROUPS - N_EXPERTS), F32)], axis=1)
    b_route = jnp.concatenate([b_router_group[0], b_router_expert[0],
                               jnp.zeros((LANES - N_GROUPS - N_EXPERTS,), F32)]).reshape(1, LANES)
    ln1g, ln1b = ln1_g[0].reshape(1, d), ln1_b[0].reshape(1, d)

    xp = x_prompt.reshape(t, d)
    xs = x_sample.reshape(ms, d)
    pp = _in_proj(xp.astype(BF16), w_bf, w_small, True)
    ps = _in_proj(xs.astype(BF16), w_bf, w_small, False)

    oa_p = _flash(pp["qa"], pp["ka_h"], pp["va_t"], None, tab, lam4, subw.reshape(2 * DH_A, 1),
                  n_maps=2, scale=1.0, head_off=0, name="flash_diff")
    qh_p = pp["qi"].reshape(t, H_I, DH_I).transpose(1, 0, 2)
    wt_p = pp["small"][:, DH_I:DH_I + H_I].T
    mask_p = _index_select(qh_p, wt_p, pp["ki_bf"], min(TOPK_MAX, t // 4))
    ob_p = _flash(pp["qb"], pp["kb_h"], pp["vb_t"], mask_p, tab, lam4, subw.reshape(2 * DH_A, 1),
                  n_maps=1, scale=DH_B ** -0.5, head_off=H_A, name="flash_dsa")

    def rows_to_batch(a):
        if a.ndim == 4:
            return a.reshape(a.shape[0], 2, nb, ts, LANES).transpose(2, 0, 1, 3, 4).reshape(nb, a.shape[0], 2 * ts, LANES)
        a = a.reshape(a.shape[0], nb, ts, LANES).transpose(1, 0, 2, 3)
        return jnp.pad(a, ((0, 0), (0, 0), (0, SROWS - ts), (0, 0)))

    def new_slab(a):
        a = jnp.pad(a.reshape(nb, ts, H_A, LANES), ((0, 0), (0, PAGE_SIZE - ts), (0, 0), (0, 0)))
        return a.reshape(nb, PAGE_SIZE * H_A, LANES)

    def new_rows(a):
        return jnp.pad(a.reshape(nb, ts, a.shape[1]), ((0, 0), (0, PAGE_SIZE - ts), (0, 0)))

    page_view = lambda c: c.reshape(n_phys, PAGE_SIZE * H_A, LANES)
    tab_rows = lambda off: jnp.repeat(tab[off:off + H_A], SROWS, axis=0)
    oa_s = _sample_attn(page_table, tab_rows(0), lam4, subw, rows_to_batch(ps["qa"]),
                        page_view(cache_k_a), page_view(cache_v_a), new_slab(ps["ka"]), new_slab(ps["va"]), None,
                        n_maps=2, scale=1.0, name="sample_diff")
    qh_s = ps["qi"].reshape(nb, ts, H_I, DH_I).transpose(0, 2, 1, 3)
    qh_s = jnp.pad(qh_s, ((0, 0), (0, 0), (0, SROWS - ts), (0, 0))).reshape(nb, H_I * SROWS, DH_I)
    wi_s = ps["small"][:, DH_I:DH_I + H_I].reshape(nb, ts, H_I).transpose(0, 2, 1) * (DH_I ** -0.5 * H_I ** -0.5)
    wi_s = jnp.pad(wi_s, ((0, 0), (0, 0), (0, SROWS - ts))).reshape(nb, H_I * SROWS, 1)
    wb_s = jnp.broadcast_to(wi_s, (nb, H_I * SROWS, LANES))
    mask_s = _sample_select(page_table, qh_s, wb_s, cache_k_idx.reshape(n_phys, PAGE_SIZE, DH_I), new_rows(ps["ki"]),
                            min(TOPK_MAX, (past_len + ts) // 4), n_pages + ATT_PAGES)
    ob_s = _sample_attn(page_table, tab_rows(H_A), lam4, subw, rows_to_batch(ps["qb"]),
                        page_view(cache_k_b), page_view(cache_v_b), new_slab(ps["kb"]), new_slab(ps["vb"]), mask_s,
                        n_maps=1, scale=DH_B ** -0.5, name="sample_dsa")
    oa_s = oa_s[:, :ts].reshape(ms, hd).astype(BF16)
    ob_s = ob_s[:, :ts].reshape(ms, hd).astype(BF16)

    n_tok = t + ms
    mg_p = _gate_merge(oa_p, ob_p, pp["ga"], pp["gb"], wa_bf, wb_bf)
    mg_s = _gate_merge(oa_s, ob_s, ps["ga"], ps["gb"], wa_bf, wb_bf)
    x1t, route_p, cnt_p = _out_ln_route(mg_p, xp, wo_bf, ln1g, ln1b, w_route, b_route, jnp.zeros((1, LANES), F32),
                                        jnp.zeros((n_tok * nc, LANES), F32), 0, n_tok)
    x1t, route_s, cnt = _out_ln_route(mg_s, xs, wo_bf, ln1g, ln1b, w_route, b_route, cnt_p, x1t, t, n_tok)
    route = jnp.concatenate([route_p, route_s], axis=0)

    counts = cnt[0, :N_EXPERTS].astype(I32)
    padded = (counts + MOE_TM - 1) // MOE_TM * MOE_TM
    ends = jnp.cumsum(padded)
    starts = ends - padded
    n_tiles = (2 * n_tok + N_EXPERTS * (MOE_TM - 1) + MOE_TM - 1) // MOE_TM
    tile_row = jnp.arange(n_tiles, dtype=I32) * MOE_TM
    tile_expert = jnp.minimum(jnp.sum(tile_row[:, None] >= ends[None, :], axis=1), N_EXPERTS - 1).astype(I32)
    tile_valid = (tile_row < ends[-1]).astype(I32)
    tile_first = ((tile_row == starts[tile_expert]) & (tile_valid == 1)).astype(I32)
    pos1 = starts[route[:, 0].astype(I32)] + route[:, 4].astype(I32)
    pos2 = starts[route[:, 1].astype(I32)] + route[:, 5].astype(I32)

    inv = _inverse_perm(pos1, pos2, n_tiles * MOE_TM)
    ysort = _expert_mlp(tile_expert, tile_valid, tile_first, inv, x1t, w_e_gate[0], w_e_up[0], w_e_down[0])
    per_row = lambda col: jnp.repeat(col, nc).reshape(-1, 1)
    w1r, w2r = per_row(route[:, 2]), per_row(route[:, 3])
    ln2g_t, ln2b_t = ln2_g[0].reshape(nc, LANES), ln2_b[0].reshape(nc, LANES)
    y_p = _combine_ln(pos1, pos2, x1t, w1r, w2r, ln2g_t, ln2b_t, ysort, nc, 0, t)
    y_s = _combine_ln(pos1, pos2, x1t, w1r, w2r, ln2g_t, ln2b_t, ysort, nc, t, ms)

    kv5 = lambda a, n: a.reshape(1, n[0], n[1], H_A, 2 * DH_A)
    return (y_p.reshape(1, t, d), y_s.reshape(nb, ts, d),
            kv5(pp["ka"], (1, t)), kv5(pp["va"], (1, t)), kv5(pp["kb"], (1, t)), kv5(pp["vb"], (1, t)),
            pp["ki"].reshape(1, 1, t, DH_I),
            kv5(ps["ka"], (nb, ts)), kv5(ps["va"], (nb, ts)), kv5(ps["kb"], (nb, ts)),
            kv5(ps["vb"], (nb, ts)), ps["ki"].reshape(1, nb, ts, DH_I))
```

```python
import functools
import math

import numpy as np
import jax
import jax.numpy as jnp
from jax import lax
from jax.experimental import pallas as pl
from jax.experimental.pallas import tpu as pltpu

F32 = jnp.float32
BF16 = jnp.bfloat16
I32 = jnp.int32

H_A = 8
DH_A = 64
H_B = 8
DH_B = 128
H_I = 16
DH_I = 64
TOPK_MAX = 256
NUM_BUCKETS = 32
MAX_DISTANCE = 128
N_GROUPS = 4
EXPERTS_PER_GROUP = 8
N_EXPERTS = N_GROUPS * EXPERTS_PER_GROUP
PAGE_SIZE = 128
DEPTH = 1
ALPHA = (2 * DEPTH) ** 0.25
LN_EPS = 1e-5
LAMBDA_INIT = 0.8 - 0.6 * math.exp(-0.3 * 0)

LANES = 128
SUBLANES = 8
NEG = -1e30
INT_MIN = -2 ** 31
KEY_NEG_INF = int(np.array([-np.inf], np.float32).view(np.int32)[0]) ^ 0x7FFFFFFF
V7X_VMEM_LIMIT = 48 * 1024 * 1024

PROJ_TM = 512
FLASH_T = 512
VT_ROWS = LANES + 16
SEL_TQ = 256
SEL_CH = 256
POST_TM = 256
MOE_TM = 256
SROWS = 8
N_NEW = 4
ATT_PAGES = 4
SEL_PAGES = 16
DMA_UNROLL = 8


def _t5_thresholds():
    n = np.arange(0, MAX_DISTANCE + 1)
    max_exact = NUM_BUCKETS // 2
    nf = np.maximum(n, 1).astype(np.float32)
    large = max_exact + (np.log(nf / max_exact) / math.log(MAX_DISTANCE / max_exact)
                         * (NUM_BUCKETS - max_exact)).astype(np.int32)
    b = np.where(n < max_exact, n, np.minimum(large, NUM_BUCKETS - 1))
    assert np.all(np.diff(b) >= 0) and b[-1] == NUM_BUCKETS - 1
    return tuple(int(np.argmax(b >= j)) for j in range(1, NUM_BUCKETS))


T5_THRESH = _t5_thresholds()
T5_FAR = T5_THRESH[-1]


def _t5_bias(d, table):
    b = jnp.broadcast_to(table(0), d.shape).astype(F32)
    for j, t in enumerate(T5_THRESH, start=1):
        b = jnp.where(d >= t, table(j), b)
    return b


def _dot_nt(a, b):
    return lax.dot_general(a, b, (((1,), (1,)), ((), ())), preferred_element_type=F32)


def _sort_key(x):
    bits = lax.bitcast_convert_type(x, I32)
    return jnp.where(bits < 0, bits ^ 0x7FFFFFFF, bits)


def _params(sem):
    return pltpu.CompilerParams(dimension_semantics=sem, vmem_limit_bytes=V7X_VMEM_LIMIT)


def _proj_kernel(x_ref, w_ref, *rest, emit, n_extra):
    extra, out_refs = rest[:n_extra], rest[n_extra:]
    res = jnp.dot(x_ref[...], w_ref[...], preferred_element_type=F32)
    emit(res, out_refs, x_ref, *extra)


def _emit_f32(res, outs, x_ref):
    outs[0][...] = res


def _emit_bf16(res, outs, x_ref):
    outs[0][...] = res.astype(BF16)


def _emit_k(res, outs, x_ref):
    outs[0][...] = res
    for h in range(res.shape[1] // LANES):
        outs[1][h] = res[:, h * LANES:(h + 1) * LANES].astype(BF16)


def _emit_v(res, outs, x_ref, wt_ref):
    outs[0][...] = res
    res_t = _dot_nt(wt_ref[...], x_ref[...])
    for h in range(res.shape[1] // LANES):
        outs[1][h, 0:LANES] = res_t[h * LANES:(h + 1) * LANES, :].astype(BF16)
        outs[1][h, LANES:VT_ROWS] = jnp.ones((VT_ROWS - LANES, res_t.shape[1]), BF16)


def _emit_heads(res, outs, x_ref):
    for h in range(res.shape[1] // LANES):
        outs[0][h] = res[:, h * LANES:(h + 1) * LANES].astype(BF16)


def _emit_qa(res, outs, x_ref):
    lane = lax.broadcasted_iota(I32, (res.shape[0], LANES), 1)
    for h in range(res.shape[1] // LANES):
        blk = res[:, h * LANES:(h + 1) * LANES] * (DH_A ** -0.5)
        outs[0][h, 0] = jnp.where(lane < DH_A, blk, 0.0).astype(BF16)
        outs[0][h, 1] = jnp.where(lane >= DH_A, blk, 0.0).astype(BF16)


def _emit_small(res, outs, x_ref):
    outs[0][...] = res
    outs[1][...] = res[:, :DH_I]
    outs[2][...] = res[:, :DH_I].astype(BF16)


def _proj(x, w, emit, out_shapes, out_blocks, tok_axes, name, extra=()):
    m, k = x.shape
    tm = min(PROJ_TM, m)

    def spec(blk, tok_axis):
        nd = len(blk)
        return pl.BlockSpec(blk, lambda i: tuple(i if a == tok_axis else 0 for a in range(nd)))

    full = lambda a: pl.BlockSpec(a.shape, lambda i: (0,) * a.ndim)
    return pl.pallas_call(
        functools.partial(_proj_kernel, emit=emit, n_extra=len(extra)),
        grid=(m // tm,),
        in_specs=[pl.BlockSpec((tm, k), lambda i: (i, 0)), full(w)] + [full(e) for e in extra],
        out_specs=[spec(b, a) for b, a in zip(out_blocks, tok_axes)],
        out_shape=out_shapes,
        compiler_params=_params(("arbitrary",)),
        name=name,
    )(x, w, *extra)


def _in_proj(x, w_bf, w_small_bf, with_attention_layouts):
    m, d = x.shape
    tm = min(PROJ_TM, m)
    hd = H_A * 2 * DH_A
    sds = jax.ShapeDtypeStruct
    cols = lambda j: w_bf[:, j * hd:(j + 1) * hd]
    out = {}
    out["qa"], = _proj(x, cols(0), _emit_qa, [sds((H_A, 2, m, LANES), BF16)], [(H_A, 2, tm, LANES)], [2], "proj_qa")
    for j, nm in ((1, "ka"), (4, "kb")):
        if with_attention_layouts:
            out[nm], out[nm + "_h"] = _proj(x, cols(j), _emit_k, [sds((m, hd), F32), sds((H_A, m, LANES), BF16)],
                                            [(tm, hd), (H_A, tm, LANES)], [0, 1], "proj_" + nm)
        else:
            out[nm], = _proj(x, cols(j), _emit_f32, [sds((m, hd), F32)], [(tm, hd)], [0], "proj_" + nm)
    for j, nm in ((2, "va"), (5, "vb")):
        if with_attention_layouts:
            out[nm], out[nm + "_t"] = _proj(x, cols(j), _emit_v, [sds((m, hd), F32), sds((H_A, VT_ROWS, m), BF16)],
                                            [(tm, hd), (H_A, VT_ROWS, tm)], [0, 2], "proj_" + nm,
                                            extra=(cols(j).T,))
        else:
            out[nm], = _proj(x, cols(j), _emit_f32, [sds((m, hd), F32)], [(tm, hd)], [0], "proj_" + nm)
    out["qb"], = _proj(x, cols(3), _emit_heads, [sds((H_B, m, LANES), BF16)], [(H_B, tm, LANES)], [1], "proj_qb")
    out["qi"], = _proj(x, cols(6), _emit_bf16, [sds((m, hd), BF16)], [(tm, hd)], [0], "proj_qi")
    g0 = 7 * hd + DH_I + H_I
    out["ga"], = _proj(x, w_bf[:, g0:g0 + d], _emit_f32, [sds((m, d), F32)], [(tm, d)], [0], "proj_ga")
    out["gb"], = _proj(x, w_bf[:, g0 + d:g0 + 2 * d], _emit_f32, [sds((m, d), F32)], [(tm, d)], [0], "proj_gb")
    out["small"], out["ki"], out["ki_bf"] = _proj(
        x, w_small_bf, _emit_small, [sds((m, LANES), F32), sds((m, DH_I), F32), sds((m, DH_I), BF16)],
        [(tm, LANES), (tm, DH_I), (tm, DH_I)], [0, 0, 0], "proj_small")
    return out


def _lambda_full(lam_ref):
    a = jnp.sum(lam_ref[0:1, :] * lam_ref[1:2, :], axis=1, keepdims=True)
    b = jnp.sum(lam_ref[2:3, :] * lam_ref[3:4, :], axis=1, keepdims=True)
    return jnp.exp(a) - jnp.exp(b) + LAMBDA_INIT


def _flash_kernel(qs_ref, ks_ref, tab_ref, lam_ref, sub_ref, q_ref, k_ref, vt_ref, *rest,
                  n_maps, tq, use_mask, scale, head_off):
    if use_mask:
        m_ref, o_ref, acc, m_s, bias_s = rest
    else:
        o_ref, acc, m_s, bias_s = rest
    h = pl.program_id(0)
    step = pl.program_id(1)
    qi = qs_ref[step]
    ki = ks_ref[step]
    cols = n_maps * tq
    hb = h + head_off
    table = lambda j: tab_ref[hb, j]

    @pl.when(step == 0)
    def _():
        r = lax.broadcasted_iota(I32, (tq, tq), 0)
        c = lax.broadcasted_iota(I32, (tq, tq), 1)
        d0 = c - r
        near = jnp.where(d0 >= 0, _t5_bias(d0, table), NEG)
        far = _t5_bias(d0 + tq, table)
        for mp in range(n_maps):
            bias_s[0, :, mp * tq:(mp + 1) * tq] = near
            bias_s[1, :, mp * tq:(mp + 1) * tq] = far

    @pl.when(ki == 0)
    def _():
        m_s[...] = jnp.full(m_s.shape, NEG, F32)
        acc[...] = jnp.zeros(acc.shape, F32)

    def update(bias, uniform):
        q = q_ref[...].reshape(cols, LANES)
        s = _dot_nt(k_ref[...], q)
        if scale != 1.0:
            s = s * scale
        if not uniform:
            s = s + bias
        if use_mask:
            s = s + m_ref[...]
        m_old = m_s[...]
        m_blk = jnp.max(s, axis=0, keepdims=True)
        if uniform:
            m_blk = m_blk + bias
        m_new = jnp.maximum(m_old, m_blk)
        alpha = jnp.exp(m_old - m_new)
        p = jnp.exp(s - ((m_new - bias) if uniform else m_new))
        acc[...] = alpha * acc[...] + jnp.dot(vt_ref[...], p.astype(BF16), preferred_element_type=F32)
        m_s[...] = m_new

    @pl.when(qi - ki >= 2)
    def _():
        update(tab_ref[hb, NUM_BUCKETS - 1], True)

    @pl.when(qi - ki < 2)
    def _():
        update(bias_s[qi - ki], False)

    @pl.when(ki == qi)
    def _():
        o = acc[0:LANES, :] / acc[LANES:LANES + 1, :]
        if n_maps == 2:
            o = o[:, :tq] - _lambda_full(lam_ref) * o[:, tq:]
            o = o * lax.rsqrt(jnp.mean(o * o, axis=0, keepdims=True) + LN_EPS) * sub_ref[...] * (1.0 - LAMBDA_INIT)
        o_ref[...] = o.T.astype(BF16)


def _flash(q, k, vt, mask_t, tab, lam4, subw_col, *, n_maps, scale, head_off, name):
    nh, t = k.shape[0], k.shape[1]
    tq = min(FLASH_T, t)
    nq = t // tq
    pairs = [(a, b) for a in range(nq) for b in range(a + 1)]
    qs = jnp.asarray([p[0] for p in pairs], I32)
    ks = jnp.asarray([p[1] for p in pairs], I32)
    cols = n_maps * tq
    smem = pl.BlockSpec(memory_space=pltpu.SMEM)
    if n_maps == 2:
        q_spec = pl.BlockSpec((None, 2, tq, LANES), lambda h, s, qs, ks: (h, 0, qs[s], 0))
    else:
        q_spec = pl.BlockSpec((None, tq, LANES), lambda h, s, qs, ks: (h, qs[s], 0))
    in_specs = [smem, pl.BlockSpec((4, DH_A), lambda h, s, qs, ks: (0, 0)),
                pl.BlockSpec((LANES, 1), lambda h, s, qs, ks: (0, 0)), q_spec,
                pl.BlockSpec((None, tq, LANES), lambda h, s, qs, ks: (h, ks[s], 0)),
                pl.BlockSpec((None, VT_ROWS, tq), lambda h, s, qs, ks: (h, 0, ks[s]))]
    args = [tab, lam4, subw_col, q, k, vt]
    if mask_t is not None:
        in_specs.append(pl.BlockSpec((tq, tq), lambda h, s, qs, ks: (ks[s], qs[s])))
        args.append(mask_t)
    grid_spec = pltpu.PrefetchScalarGridSpec(
        num_scalar_prefetch=2,
        grid=(nh, len(pairs)),
        in_specs=in_specs,
        out_specs=pl.BlockSpec((tq, LANES), lambda h, s, qs, ks: (qs[s], h)),
        scratch_shapes=[pltpu.VMEM((VT_ROWS, cols), F32), pltpu.VMEM((1, cols), F32),
                        pltpu.VMEM((2, tq, cols), F32)],
    )
    return pl.pallas_call(
        functools.partial(_flash_kernel, n_maps=n_maps, tq=tq, use_mask=mask_t is not None, scale=scale,
                          head_off=head_off),
        grid_spec=grid_spec,
        out_shape=jax.ShapeDtypeStruct((t, nh * LANES), BF16),
        compiler_params=_params(("arbitrary", "arbitrary")),
        name=name,
    )(qs, ks, *args)


def _kth_largest_key(count_ge, shape, k_top):
    cand0 = jnp.zeros(shape, I32)
    res = jnp.where(count_ge(cand0) >= k_top, cand0, jnp.full(shape, INT_MIN, I32))

    def bit_body(b, res):
        cand = res + jnp.left_shift(jnp.int32(1), jnp.int32(30) - b)
        return jnp.where(count_ge(cand) >= k_top, cand, res)

    return lax.fori_loop(0, 31, bit_body, res)


def _index_select_kernel(qh_ref, wt_ref, ki_ref, o_ref, keys_s, *, tq, ch, n_ch, k_top):
    i = pl.program_id(0)
    q0 = i * tq
    n_valid = (q0 + tq + ch - 1) // ch
    wt = wt_ref[...] * (DH_I ** -0.5 * H_I ** -0.5)
    krow = lax.broadcasted_iota(I32, (ch, tq), 0)
    qpos = lax.broadcasted_iota(I32, (ch, tq), 1) + q0

    def score_chunk(c, _):
        k0 = pl.multiple_of(c * ch, ch)
        kc = ki_ref[pl.ds(k0, ch), :]
        acc = jnp.zeros((ch, tq), F32)
        for h in range(H_I):
            acc = acc + wt[h:h + 1, :] * jnp.maximum(_dot_nt(kc, qh_ref[h]), 0.0)
        val = jnp.where(krow + k0 <= qpos, acc, -jnp.inf)
        keys_s[pl.ds(k0, ch), :] = _sort_key(val)
        return 0

    lax.fori_loop(0, n_valid, score_chunk, 0)

    def count_ge(cand):
        def chunk(c, a):
            blk = keys_s[pl.ds(pl.multiple_of(c * ch, ch), ch), :]
            hit = (blk >= cand[0:1, :]).astype(I32)
            return a + jnp.sum(hit.reshape(ch // SUBLANES, SUBLANES, tq), axis=0)
        a = lax.fori_loop(0, n_valid, chunk, jnp.zeros((SUBLANES, tq), I32))
        return jnp.sum(a.astype(F32), axis=0, keepdims=True)

    thr = _kth_largest_key(count_ge, (SUBLANES, tq), k_top)[0:1, :]

    def write_chunk(c, _):
        k0 = pl.multiple_of(c * ch, ch)
        key = keys_s[pl.ds(k0, ch), :]
        o_ref[pl.ds(k0, ch), :] = jnp.where((key >= thr) & (key > KEY_NEG_INF), 0.0, NEG)
        return 0

    def fill_chunk(c, _):
        o_ref[pl.ds(pl.multiple_of(c * ch, ch), ch), :] = jnp.full((ch, tq), NEG, F32)
        return 0

    lax.fori_loop(0, n_valid, write_chunk, 0)
    lax.fori_loop(n_valid, n_ch, fill_chunk, 0)


def _index_select(qh, wt, ki_bf, k_top):
    t = ki_bf.shape[0]
    tq = min(SEL_TQ, t)
    ch = min(SEL_CH, t)
    return pl.pallas_call(
        functools.partial(_index_select_kernel, tq=tq, ch=ch, n_ch=t // ch, k_top=k_top),
        grid=(t // tq,),
        in_specs=[pl.BlockSpec((H_I, tq, DH_I), lambda i: (0, i, 0)),
                  pl.BlockSpec((H_I, tq), lambda i: (0, i)),
                  pl.BlockSpec((t, DH_I), lambda i: (0, 0))],
        out_specs=pl.BlockSpec((t, tq), lambda i: (0, i)),
        out_shape=jax.ShapeDtypeStruct((t, t), F32),
        scratch_shapes=[pltpu.VMEM((t, tq), I32)],
        compiler_params=_params(("arbitrary",)),
        name="index_select",
    )(qh, wt, ki_bf)


def _sample_select_kernel(pt_ref, q_ref, wb_ref, *rest, n_pages, n_out, k_top, n_rep):
    kc_refs, (kn_ref, o_ref, keys_s) = rest[:SEL_PAGES], rest[SEL_PAGES:]
    j = pl.program_id(1)
    n_steps = n_pages // SEL_PAGES
    row = lax.broadcasted_iota(I32, (SROWS, LANES), 0)
    lane = lax.broadcasted_iota(I32, (SROWS, LANES), 1)
    tok = row % N_NEW

    def score(k_f32):
        s = jnp.maximum(_dot_nt(q_ref[...], k_f32.astype(BF16)), 0.0) * wb_ref[...]
        return jnp.sum(s.reshape(H_I, SROWS, LANES), axis=0)

    @pl.when(j < n_steps)
    def _():
        for g in range(SEL_PAGES):
            keys_s[j * SEL_PAGES + g] = _sort_key(score(kc_refs[g][...]))

    @pl.when(j == n_steps)
    def _():
        val = jnp.where((lane <= tok) & (lane < N_NEW), score(kn_ref[...]), -jnp.inf)
        keys_s[n_pages] = _sort_key(val)
        keys_s[n_pages + 1] = jnp.full((SROWS, LANES), INT_MIN, I32)
        keys = keys_s[...]

        def count_ge(cand):
            hit = (keys >= cand[None]).astype(I32)
            return jnp.sum(jnp.sum(hit, axis=0).astype(F32), axis=1, keepdims=True)

        thr = _kth_largest_key(count_ge, (SROWS, LANES), k_top)
        sel = jnp.where((keys >= thr[None]) & (keys > KEY_NEG_INF), 1.0, 0.0)
        sel = sel.reshape((n_pages + 2) * SROWS, LANES).astype(BF16)
        pos = lax.broadcasted_iota(I32, (LANES, LANES * n_rep), 0)
        rep_lane = lax.broadcasted_iota(I32, (LANES, LANES * n_rep), 1)
        expand = (jnp.right_shift(rep_lane, int(math.log2(n_rep))) == pos).astype(BF16)
        wide = jnp.dot(sel, expand, preferred_element_type=F32)
        o_ref[0:n_pages + 2] = jnp.where(wide > 0.5, 0.0, NEG).reshape(n_pages + 2, SROWS, LANES * n_rep)
        for c in range(n_pages + 2, n_out):
            o_ref[c] = jnp.full((SROWS, LANES * n_rep), NEG, F32)


def _sample_select(page_table, qh, wb, cache_k_idx, ki_new, k_top, n_out, n_rep):
    nb, n_pages = page_table.shape
    assert n_pages % SEL_PAGES == 0 and n_out >= n_pages + 2 and n_rep & (n_rep - 1) == 0

    def page_spec(g):
        return pl.BlockSpec((None, PAGE_SIZE, DH_I),
                            lambda b, j, pt: (pt[b, jnp.minimum(j * SEL_PAGES + g, n_pages - 1)], 0, 0))

    grid_spec = pltpu.PrefetchScalarGridSpec(
        num_scalar_prefetch=1,
        grid=(nb, n_pages // SEL_PAGES + 1),
        in_specs=[pl.BlockSpec((None, H_I * SROWS, DH_I), lambda b, j, pt: (b, 0, 0)),
                  pl.BlockSpec((None, H_I * SROWS, LANES), lambda b, j, pt: (b, 0, 0))]
                 + [page_spec(g) for g in range(SEL_PAGES)]
                 + [pl.BlockSpec((None, PAGE_SIZE, DH_I), lambda b, j, pt: (b, 0, 0))],
        out_specs=pl.BlockSpec((None, n_out, SROWS, LANES * n_rep), lambda b, j, pt: (b, 0, 0, 0)),
        scratch_shapes=[pltpu.VMEM((n_pages + 2, SROWS, LANES), I32)],
    )
    return pl.pallas_call(
        functools.partial(_sample_select_kernel, n_pages=n_pages, n_out=n_out, k_top=k_top, n_rep=n_rep),
        grid_spec=grid_spec,
        out_shape=jax.ShapeDtypeStruct((nb, n_out, SROWS, LANES * n_rep), F32),
        compiler_params=_params(("arbitrary", "arbitrary")),
        name="sample_select",
    )(page_table, qh, wb, *([cache_k_idx] * SEL_PAGES), ki_new)


def _sample_attn_kernel(pt_ref, tabr_ref, lam_ref, sub_ref, q_ref, *rest,
                        n_maps, use_mask, scale, n_pages, n_heads):
    g_n = ATT_PAGES
    kc_refs, vc_refs = rest[:g_n], rest[g_n:2 * g_n]
    rest = rest[2 * g_n:]
    if use_mask:
        kn_ref, vn_ref, m_ref, o_ref, acc, m_s, l_s, far_s = rest
    else:
        kn_ref, vn_ref, o_ref, acc, m_s, l_s, far_s = rest
    j = pl.program_id(1)
    n_steps = n_pages // g_n
    past_len = n_pages * PAGE_SIZE
    rows = n_heads * SROWS
    wide = n_heads * PAGE_SIZE
    head_shift = int(math.log2(n_heads))
    table = lambda b: tabr_ref[:, b:b + 1]

    def geometry():
        row = lax.broadcasted_iota(I32, (rows, wide), 0)
        lane = lax.broadcasted_iota(I32, (rows, wide), 1)
        own = (lane & (n_heads - 1)) == jnp.right_shift(row, int(math.log2(SROWS)))
        return row % N_NEW, jnp.right_shift(lane, head_shift), own

    @pl.when(j == 0)
    def _():
        m_s[...] = jnp.full(m_s.shape, NEG, F32)
        l_s[...] = jnp.zeros(l_s.shape, F32)
        acc[...] = jnp.zeros(acc.shape, F32)
        far_s[...] = jnp.where(geometry()[2], table(NUM_BUCKETS - 1), NEG)

    def update(k_refs, v_refs, biases, page_masks):
        parts = []
        for k_ref, bias, page_mask in zip(k_refs, biases, page_masks):
            s = _dot_nt(q_ref[...], k_ref[...].astype(BF16))
            if scale != 1.0:
                s = s * scale
            s = s + bias
            if page_mask is not None:
                s = s + jnp.concatenate([page_mask] * n_heads, axis=0)
            parts.append(s)
        m_old = m_s[...]
        m_blk = jnp.max(parts[0], axis=1, keepdims=True)
        for s in parts[1:]:
            m_blk = jnp.maximum(m_blk, jnp.max(s, axis=1, keepdims=True))
        m_new = jnp.maximum(m_old, m_blk)
        alpha = jnp.exp(m_old - m_new)
        l_new = alpha * l_s[...]
        acc_new = alpha * acc[...]
        for s, v_ref in zip(parts, v_refs):
            pr = jnp.exp(s - m_new)
            l_new = l_new + jnp.sum(pr, axis=1, keepdims=True)
            acc_new = acc_new + jnp.dot(pr.astype(BF16), v_ref[...].astype(BF16), preferred_element_type=F32)
        l_s[...] = l_new
        acc[...] = acc_new
        m_s[...] = m_new

    far_steps = (past_len - T5_FAR - PAGE_SIZE + 1) // (PAGE_SIZE * g_n)
    step_masks = lambda: [m_ref[g] if use_mask else None for g in range(g_n)]

    @pl.when(j < far_steps)
    def _():
        far = far_s[...]
        update(kc_refs, vc_refs, [far] * g_n, step_masks())

    @pl.when((j >= far_steps) & (j < n_steps))
    def _():
        tok, pos, own = geometry()
        biases = []
        for g in range(g_n):
            dist = (past_len + tok) - ((j * g_n + g) * PAGE_SIZE + pos)
            biases.append(jnp.where(own, _t5_bias(dist, table), NEG))
        update(kc_refs, vc_refs, biases, step_masks())

    @pl.when(j == n_steps)
    def _():
        tok, pos, own = geometry()
        dist = tok - pos
        bias = jnp.where(own & (dist >= 0) & (pos < N_NEW), _t5_bias(dist, table), NEG)
        update([kn_ref], [vn_ref], [bias], [m_ref[0] if use_mask else None])
        o = acc[...] / l_s[...]
        for h in range(n_heads):
            oh = o[h * SROWS:(h + 1) * SROWS]
            if n_maps == 2:
                oh = oh - _lambda_full(lam_ref) * pltpu.roll(oh, N_NEW, 0)
                oh = oh * lax.rsqrt(jnp.mean(oh * oh, axis=1, keepdims=True) + LN_EPS) * sub_ref[...] \
                    * (1.0 - LAMBDA_INIT)
            o_ref[:, h * LANES:(h + 1) * LANES] = oh


def _sample_attn(page_table, tab_rows, lam4, subw, q, cache_k, cache_v, k_new, v_new, mask, *, n_maps, scale, name):
    nb, n_pages = page_table.shape
    assert n_pages % ATT_PAGES == 0
    nh = q.shape[1] // SROWS
    assert nh & (nh - 1) == 0
    slab = PAGE_SIZE * nh

    def page_spec(g):
        return pl.BlockSpec((None, slab, LANES),
                            lambda b, j, pt: (pt[b, jnp.minimum(j * ATT_PAGES + g, n_pages - 1)], 0, 0))

    new_spec = pl.BlockSpec((None, slab, LANES), lambda b, j, pt: (b, 0, 0))
    in_specs = [pl.BlockSpec(tab_rows.shape, lambda b, j, pt: (0, 0)),
                pl.BlockSpec((4, DH_A), lambda b, j, pt: (0, 0)),
                pl.BlockSpec((1, LANES), lambda b, j, pt: (0, 0)),
                pl.BlockSpec((None, nh * SROWS, LANES), lambda b, j, pt: (b, 0, 0))]
    in_specs += [page_spec(g) for g in range(ATT_PAGES)] * 2 + [new_spec, new_spec]
    args = [tab_rows, lam4, subw, q] + [cache_k] * ATT_PAGES + [cache_v] * ATT_PAGES + [k_new, v_new]
    if mask is not None:
        in_specs.append(pl.BlockSpec((None, ATT_PAGES, SROWS, slab), lambda b, j, pt: (b, j, 0, 0)))
        args.append(mask)
    grid_spec = pltpu.PrefetchScalarGridSpec(
        num_scalar_prefetch=1,
        grid=(nb, n_pages // ATT_PAGES + 1),
        in_specs=in_specs,
        out_specs=pl.BlockSpec((None, SROWS, nh * LANES), lambda b, j, pt: (b, 0, 0)),
        scratch_shapes=[pltpu.VMEM((nh * SROWS, LANES), F32), pltpu.VMEM((nh * SROWS, 1), F32),
                        pltpu.VMEM((nh * SROWS, 1), F32), pltpu.VMEM((nh * SROWS, slab), F32)],
    )
    return pl.pallas_call(
        functools.partial(_sample_attn_kernel, n_maps=n_maps, use_mask=mask is not None, scale=scale,
                          n_pages=n_pages, n_heads=nh),
        grid_spec=grid_spec,
        out_shape=jax.ShapeDtypeStruct((nb, SROWS, nh * LANES), F32),
        compiler_params=_params(("arbitrary", "arbitrary")),
        name=name,
    )(page_table, *args)


def _gate_merge_kernel(oa_ref, ob_ref, ga_ref, gb_ref, wa_ref, wb_ref, o_ref):
    a = jnp.dot(oa_ref[...], wa_ref[...], preferred_element_type=F32)
    b = jnp.dot(ob_ref[...], wb_ref[...], preferred_element_type=F32)
    o_ref[...] = (jax.nn.sigmoid(ga_ref[...]) * a + jax.nn.sigmoid(gb_ref[...]) * b).astype(BF16)


def _gate_merge(oa, ob, ga, gb, wa, wb):
    m, d = ga.shape
    tm = min(POST_TM, m)
    row = lambda w: pl.BlockSpec((tm, w), lambda i: (i, 0))
    full = lambda a: pl.BlockSpec(a.shape, lambda i: (0, 0))
    return pl.pallas_call(
        _gate_merge_kernel,
        grid=(m // tm,),
        in_specs=[row(oa.shape[1]), row(ob.shape[1]), row(d), row(d), full(wa), full(wb)],
        out_specs=row(d),
        out_shape=jax.ShapeDtypeStruct((m, d), BF16),
        compiler_params=_params(("arbitrary",)),
        name="gate_merge",
    )(oa, ob, ga, gb, wa, wb)


def _store_token_major(ref, x):
    tm, d = x.shape
    nc = d // LANES
    for c in range(nc):
        ref[pl.ds(c, tm, stride=nc), :] = x[:, c * LANES:(c + 1) * LANES]


def _load_token_major(ref, start, tm, nc, dtype):
    return jnp.concatenate([ref[pl.ds(start + c, tm, stride=nc), :].astype(dtype) for c in range(nc)], axis=1)


def _layer_norm(x, g_ref, b_ref):
    mu = jnp.mean(x, axis=1, keepdims=True)
    xc = x - mu
    var = jnp.mean(xc * xc, axis=1, keepdims=True)
    return xc * lax.rsqrt(var + LN_EPS) * g_ref[...] + b_ref[...]


def _out_ln_route_kernel(mg_ref, x_ref, wo_ref, g_ref, b_ref, wr_ref, br_ref, cin_ref, *rest, tm):
    x1t_ref, route_ref, cnt_ref, carry = rest[-4:]
    i = pl.program_id(0)

    @pl.when(i == 0)
    def _():
        carry[...] = cin_ref[...]

    mix = jnp.dot(mg_ref[...], wo_ref[...], preferred_element_type=F32)
    x1 = _layer_norm(ALPHA * x_ref[...] + mix, g_ref, b_ref)
    _store_token_major(x1t_ref, x1)

    z = jnp.dot(x1, wr_ref[...], preferred_element_type=F32, precision=lax.Precision.HIGHEST) + br_ref[...]
    lane = lax.broadcasted_iota(I32, (tm, LANES), 1).astype(F32)
    ninf = -jnp.inf
    big = jnp.float32(2 ** 30)
    gl = jnp.where(lane < N_GROUPS, z, ninf)
    gmax = jnp.max(gl, axis=1, keepdims=True)
    gsel = jnp.min(jnp.where(gl == gmax, lane, big), axis=1, keepdims=True)
    p_g = 1.0 / jnp.sum(jnp.exp(gl - gmax), axis=1, keepdims=True)
    e_lane = lane - N_GROUPS
    in_grp = (e_lane >= gsel * EXPERTS_PER_GROUP) & (e_lane < (gsel + 1) * EXPERTS_PER_GROUP)
    el = jnp.where(in_grp, z, ninf)
    v1 = jnp.max(el, axis=1, keepdims=True)
    i1 = jnp.min(jnp.where(el == v1, lane, big), axis=1, keepdims=True)
    el2 = jnp.where(lane == i1, ninf, el)
    v2 = jnp.max(el2, axis=1, keepdims=True)
    i2 = jnp.min(jnp.where(el2 == v2, lane, big), axis=1, keepdims=True)
    e2x = jnp.exp(v2 - v1)
    w1 = p_g / (1.0 + e2x)
    w2 = p_g * e2x / (1.0 + e2x)
    e1 = i1 - N_GROUPS
    e2 = i2 - N_GROUPS

    hot1 = lane == e1
    hot2 = lane == e2
    onehot = (hot1 | hot2).astype(BF16)
    r = lax.broadcasted_iota(I32, (tm, tm), 0)
    c = lax.broadcasted_iota(I32, (tm, tm), 1)
    tri = (r > c).astype(BF16)
    prefix = jnp.dot(tri, onehot, preferred_element_type=F32) + carry[...]
    rank1 = jnp.sum(jnp.where(hot1, prefix, 0.0), axis=1, keepdims=True)
    rank2 = jnp.sum(jnp.where(hot2, prefix, 0.0), axis=1, keepdims=True)
    carry[...] = carry[...] + jnp.sum(onehot.astype(F32), axis=0, keepdims=True)
    cnt_ref[...] = carry[...]

    route = jnp.zeros((tm, LANES), F32)
    for j, col in enumerate((e1, e2, w1, w2, rank1, rank2)):
        route = jnp.where(lane == j, col, route)
    route_ref[...] = route


def _out_ln_route(merged, x, wo, g, b, wr, br, carry_in, x1t_prev, tok_off, n_tok_total):
    m, d = x.shape
    tm = min(POST_TM, m)
    nc = d // LANES
    assert tok_off % tm == 0
    blk_off = tok_off // tm
    row = lambda w: pl.BlockSpec((tm, w), lambda i: (i, 0))
    full = lambda a: pl.BlockSpec(a.shape, lambda i: (0, 0))
    sds = jax.ShapeDtypeStruct
    in_specs = [row(d), row(d), full(wo), full(g), full(b), full(wr), full(br), full(carry_in)]
    args = [merged, x, wo, g, b, wr, br, carry_in]
    aliases = {}
    if x1t_prev is not None:
        in_specs.append(pl.BlockSpec(memory_space=pl.ANY))
        args.append(x1t_prev)
        aliases = {len(args) - 1: 0}
    return pl.pallas_call(
        functools.partial(_out_ln_route_kernel, tm=tm),
        grid=(m // tm,),
        in_specs=in_specs,
        out_specs=[pl.BlockSpec((tm * nc, LANES), lambda i: (i + blk_off, 0)), row(LANES),
                   pl.BlockSpec((1, LANES), lambda i: (0, 0))],
        out_shape=[sds((n_tok_total * nc, LANES), F32), sds((m, LANES), F32), sds((1, LANES), F32)],
        scratch_shapes=[pltpu.VMEM((1, LANES), F32)],
        input_output_aliases=aliases,
        compiler_params=_params(("arbitrary",)),
        name="out_ln_route",
    )(*args)


def _inverse_perm_kernel(pos1_ref, pos2_ref, inv_ref, *, n_tok, n_slots):
    def zero(s, _):
        inv_ref[s] = 0
        return 0

    def put(t, _):
        inv_ref[pos1_ref[t]] = t
        inv_ref[pos2_ref[t]] = t
        return 0

    lax.fori_loop(0, n_slots, zero, 0, unroll=DMA_UNROLL)
    lax.fori_loop(0, n_tok, put, 0, unroll=DMA_UNROLL)


def _inverse_perm(pos1, pos2, n_slots):
    smem = pl.BlockSpec(memory_space=pltpu.SMEM)
    return pl.pallas_call(
        functools.partial(_inverse_perm_kernel, n_tok=pos1.shape[0], n_slots=n_slots),
        in_specs=[smem, smem],
        out_specs=smem,
        out_shape=jax.ShapeDtypeStruct((n_slots,), I32),
        name="inverse_perm",
    )(pos1, pos2)


def _expert_mlp_kernel(te_ref, tv_ref, tf_ref, inv_ref, x_hbm, wg_ref, wu_ref, wd_ref, o_ref,
                       wg_s, wu_s, wd_s, xbuf, sem, *, n_tiles, nc):
    i = pl.program_id(0)
    slab = MOE_TM * nc

    def row_copy(tile, r):
        slot = tile % 2
        src = x_hbm.at[pl.ds(pl.multiple_of(inv_ref[tile * MOE_TM + r] * nc, nc), nc)]
        dst = xbuf.at[pl.ds(pl.multiple_of(slot * slab + r * nc, nc), nc)]
        return pltpu.make_async_copy(src, dst, sem.at[slot])

    def start_gather(tile):
        lax.fori_loop(0, MOE_TM, lambda r, _: (row_copy(tile, r).start(), 0)[1], 0, unroll=DMA_UNROLL)

    def wait_gather(tile):
        lax.fori_loop(0, MOE_TM, lambda r, _: (row_copy(tile, r).wait(), 0)[1], 0, unroll=DMA_UNROLL)

    @pl.when((i == 0) & (tv_ref[0] == 1))
    def _():
        start_gather(0)

    @pl.when((i + 1 < n_tiles) & (tv_ref[jnp.minimum(i + 1, n_tiles - 1)] == 1))
    def _():
        start_gather(i + 1)

    @pl.when(tf_ref[i] == 1)
    def _():
        wg_s[...] = wg_ref[...].astype(BF16)
        wu_s[...] = wu_ref[...].astype(BF16)
        wd_s[...] = wd_ref[...].astype(BF16)

    @pl.when(tv_ref[i] == 1)
    def _():
        wait_gather(i)
        x = _load_token_major(xbuf, (i % 2) * slab, MOE_TM, nc, BF16)
        g = jnp.dot(x, wg_s[...], preferred_element_type=F32)
        u = jnp.dot(x, wu_s[...], preferred_element_type=F32)
        hid = (jax.nn.silu(g) * u).astype(BF16)
        _store_token_major(o_ref, jnp.dot(hid, wd_s[...], preferred_element_type=F32))

    @pl.when(tv_ref[i] == 0)
    def _():
        o_ref[...] = jnp.zeros(o_ref.shape, F32)


def _expert_mlp(tile_expert, tile_valid, tile_first, inv, x1t, w_gate, w_up, w_down):
    d, ff = w_gate.shape[1], w_gate.shape[2]
    nc = d // LANES
    n_tiles = inv.shape[0] // MOE_TM
    grid_spec = pltpu.PrefetchScalarGridSpec(
        num_scalar_prefetch=4, grid=(n_tiles,),
        in_specs=[pl.BlockSpec(memory_space=pl.ANY),
                  pl.BlockSpec((None, d, ff), lambda i, te, tv, tf, inv: (te[i], 0, 0)),
                  pl.BlockSpec((None, d, ff), lambda i, te, tv, tf, inv: (te[i], 0, 0)),
                  pl.BlockSpec((None, ff, d), lambda i, te, tv, tf, inv: (te[i], 0, 0))],
        out_specs=pl.BlockSpec((MOE_TM * nc, LANES), lambda i, te, tv, tf, inv: (i, 0)),
        scratch_shapes=[pltpu.VMEM((d, ff), BF16), pltpu.VMEM((d, ff), BF16), pltpu.VMEM((ff, d), BF16),
                        pltpu.VMEM((2 * MOE_TM * nc, LANES), F32), pltpu.SemaphoreType.DMA((2,))])
    return pl.pallas_call(
        functools.partial(_expert_mlp_kernel, n_tiles=n_tiles, nc=nc),
        grid_spec=grid_spec,
        out_shape=jax.ShapeDtypeStruct((inv.shape[0] * nc, LANES), F32),
        compiler_params=_params(("arbitrary",)),
        name="expert_mlp",
    )(tile_expert, tile_valid, tile_first, inv, x1t, w_gate, w_up, w_down)


def _combine_ln_kernel(pos1_ref, pos2_ref, x1t_ref, w1_ref, w2_ref, g_ref, b_ref, ys_hbm, o_ref, g1, g2, y_s, sem,
                       *, tm, nc, tok_off):
    base = tok_off + pl.program_id(0) * tm

    def row_copy(r, pos_ref, dst, slot):
        src = ys_hbm.at[pl.ds(pl.multiple_of(pos_ref[base + r] * nc, nc), nc)]
        return pltpu.make_async_copy(src, dst.at[pl.ds(pl.multiple_of(r * nc, nc), nc)], sem.at[slot])

    def start(r, _):
        row_copy(r, pos1_ref, g1, 0).start()
        row_copy(r, pos2_ref, g2, 1).start()
        return 0

    def wait(r, _):
        row_copy(r, pos1_ref, g1, 0).wait()
        row_copy(r, pos2_ref, g2, 1).wait()
        return 0

    lax.fori_loop(0, tm, start, 0, unroll=DMA_UNROLL)
    lax.fori_loop(0, tm, wait, 0, unroll=DMA_UNROLL)
    v = ALPHA * x1t_ref[...] + w1_ref[...] * g1[...] + w2_ref[...] * g2[...]
    v = v.reshape(tm, nc, LANES)
    tok_mean = lambda a: jnp.sum(jnp.sum(a, axis=2, keepdims=True), axis=1, keepdims=True) / (nc * LANES)
    xc = v - tok_mean(v)
    y = xc * lax.rsqrt(tok_mean(xc * xc) + LN_EPS) * g_ref[...][None] + b_ref[...][None]
    y_s[...] = y.reshape(tm * nc, LANES)
    for c in range(nc):
        o_ref[:, c * LANES:(c + 1) * LANES] = y_s[pl.ds(c, tm, stride=nc), :]


def _combine_ln(pos1, pos2, x1t, w1, w2, g, b, ys, nc, tok_off, m):
    d = nc * LANES
    tm = min(POST_TM, m)
    assert tok_off % tm == 0
    blk_off = tok_off // tm
    tok = lambda w: pl.BlockSpec((tm * nc, w), lambda i, p1, p2: (i + blk_off, 0))
    full = lambda a: pl.BlockSpec(a.shape, lambda i, p1, p2: (0, 0))
    grid_spec = pltpu.PrefetchScalarGridSpec(
        num_scalar_prefetch=2, grid=(m // tm,),
        in_specs=[tok(LANES), tok(1), tok(1), full(g), full(b), pl.BlockSpec(memory_space=pl.ANY)],
        out_specs=pl.BlockSpec((tm, d), lambda i, p1, p2: (i, 0)),
        scratch_shapes=[pltpu.VMEM((tm * nc, LANES), F32), pltpu.VMEM((tm * nc, LANES), F32),
                        pltpu.VMEM((tm * nc, LANES), F32), pltpu.SemaphoreType.DMA((2,))])
    return pl.pallas_call(
        functools.partial(_combine_ln_kernel, tm=tm, nc=nc, tok_off=tok_off),
        grid_spec=grid_spec,
        out_shape=jax.ShapeDtypeStruct((m, d), F32),
        compiler_params=_params(("arbitrary",)),
        name="combine_ln",
    )(pos1, pos2, x1t, w1, w2, g, b, ys)


def kernel(x_prompt, x_sample, cache_k_a, cache_v_a, cache_k_b, cache_v_b, cache_k_idx, page_table, w_in,
           lambda_q1, lambda_k1, lambda_q2, lambda_k2, subln_w, w_br_a, w_br_b, w_o, rel_bias, ln1_g, ln1_b,
           w_router_group, b_router_group, w_router_expert, b_router_expert, w_e_gate, w_e_up, w_e_down,
           ln2_g, ln2_b):
    assert w_in.shape[0] == DEPTH
    bp, t, d = x_prompt.shape
    assert bp == 1
    nb, ts, _ = x_sample.shape
    assert ts == N_NEW
    n_pages = page_table.shape[1]
    past_len = n_pages * PAGE_SIZE
    hd = H_A * 2 * DH_A
    n_phys = cache_k_a.shape[1]
    ms = nb * ts
    nc = d // LANES

    w = w_in[0]
    w_bf = w.astype(BF16)
    s0 = 7 * hd
    w_small = jnp.concatenate([w[:, s0:s0 + DH_I + H_I], jnp.zeros((d, LANES - DH_I - H_I), F32)], axis=1).astype(BF16)
    tab = rel_bias.T.astype(F32)
    lam4 = jnp.stack([lambda_q1[0], lambda_k1[0], lambda_q2[0], lambda_k2[0]]).astype(F32)
    subw = subln_w[0].reshape(1, 2 * DH_A).astype(F32)
    wa_bf = w_br_a[0].astype(BF16)
    wb_bf = w_br_b[0].astype(BF16)
    wo_bf = w_o[0].astype(BF16)
    w_route = jnp.concatenate([w_router_group[0], w_router_expert[0],
                               jnp.zeros((d, LANES - N_GROUPS - N_EXPERTS), F32)], axis=1)
    b_route = jnp.concatenate([b_router_group[0], b_router_expert[0],
                               jnp.zeros((LANES - N_GROUPS - N_EXPERTS,), F32)]).reshape(1, LANES)
    ln1g, ln1b = ln1_g[0].reshape(1, d), ln1_b[0].reshape(1, d)

    xp = x_prompt.reshape(t, d)
    xs = x_sample.reshape(ms, d)
    pp = _in_proj(xp.astype(BF16), w_bf, w_small, True)
    ps = _in_proj(xs.astype(BF16), w_bf, w_small, False)

    oa_p = _flash(pp["qa"], pp["ka_h"], pp["va_t"], None, tab, lam4, subw.reshape(2 * DH_A, 1),
                  n_maps=2, scale=1.0, head_off=0, name="flash_diff")
    qh_p = pp["qi"].reshape(t, H_I, DH_I).transpose(1, 0, 2)
    wt_p = pp["small"][:, DH_I:DH_I + H_I].T
    mask_p = _index_select(qh_p, wt_p, pp["ki_bf"], min(TOPK_MAX, t // 4))
    ob_p = _flash(pp["qb"], pp["kb_h"], pp["vb_t"], mask_p, tab, lam4, subw.reshape(2 * DH_A, 1),
                  n_maps=1, scale=DH_B ** -0.5, head_off=H_A, name="flash_dsa")

    def rows_to_batch(a):
        nh = a.shape[0]
        if a.ndim == 4:
            return a.reshape(nh, 2, nb, ts, LANES).transpose(2, 0, 1, 3, 4).reshape(nb, nh * 2 * ts, LANES)
        a = a.reshape(nh, nb, ts, LANES).transpose(1, 0, 2, 3)
        return jnp.pad(a, ((0, 0), (0, 0), (0, SROWS - ts), (0, 0))).reshape(nb, nh * SROWS, LANES)

    def new_slab(a):
        a = jnp.pad(a.reshape(nb, ts, H_A, LANES), ((0, 0), (0, PAGE_SIZE - ts), (0, 0), (0, 0)))
        return a.reshape(nb, PAGE_SIZE * H_A, LANES)

    def new_rows(a):
        return jnp.pad(a.reshape(nb, ts, a.shape[1]), ((0, 0), (0, PAGE_SIZE - ts), (0, 0)))

    page_view = lambda c: c.reshape(n_phys, PAGE_SIZE * H_A, LANES)
    tab_rows = lambda off: jnp.repeat(tab[off:off + H_A], SROWS, axis=0)
    oa_s = _sample_attn(page_table, tab_rows(0), lam4, subw, rows_to_batch(ps["qa"]),
                        page_view(cache_k_a), page_view(cache_v_a), new_slab(ps["ka"]), new_slab(ps["va"]), None,
                        n_maps=2, scale=1.0, name="sample_diff")
    qh_s = ps["qi"].reshape(nb, ts, H_I, DH_I).transpose(0, 2, 1, 3)
    qh_s = jnp.pad(qh_s, ((0, 0), (0, 0), (0, SROWS - ts), (0, 0))).reshape(nb, H_I * SROWS, DH_I)
    wi_s = ps["small"][:, DH_I:DH_I + H_I].reshape(nb, ts, H_I).transpose(0, 2, 1) * (DH_I ** -0.5 * H_I ** -0.5)
    wi_s = jnp.pad(wi_s, ((0, 0), (0, 0), (0, SROWS - ts))).reshape(nb, H_I * SROWS, 1)
    wb_s = jnp.broadcast_to(wi_s, (nb, H_I * SROWS, LANES))
    mask_s = _sample_select(page_table, qh_s, wb_s, cache_k_idx.reshape(n_phys, PAGE_SIZE, DH_I), new_rows(ps["ki"]),
                            min(TOPK_MAX, (past_len + ts) // 4), n_pages + ATT_PAGES, H_B)
    ob_s = _sample_attn(page_table, tab_rows(H_A), lam4, subw, rows_to_batch(ps["qb"]),
                        page_view(cache_k_b), page_view(cache_v_b), new_slab(ps["kb"]), new_slab(ps["vb"]), mask_s,
                        n_maps=1, scale=DH_B ** -0.5, name="sample_dsa")
    oa_s = oa_s[:, :ts].reshape(ms, hd).astype(BF16)
    ob_s = ob_s[:, :ts].reshape(ms, hd).astype(BF16)

    n_tok = t + ms
    mg_p = _gate_merge(oa_p, ob_p, pp["ga"], pp["gb"], wa_bf, wb_bf)
    mg_s = _gate_merge(oa_s, ob_s, ps["ga"], ps["gb"], wa_bf, wb_bf)
    x1t, route_p, cnt_p = _out_ln_route(mg_p, xp, wo_bf, ln1g, ln1b, w_route, b_route, jnp.zeros((1, LANES), F32),
                                        jnp.zeros((n_tok * nc, LANES), F32), 0, n_tok)
    x1t, route_s, cnt = _out_ln_route(mg_s, xs, wo_bf, ln1g, ln1b, w_route, b_route, cnt_p, x1t, t, n_tok)
    route = jnp.concatenate([route_p, route_s], axis=0)

    counts = cnt[0, :N_EXPERTS].astype(I32)
    padded = (counts + MOE_TM - 1) // MOE_TM * MOE_TM
    ends = jnp.cumsum(padded)
    starts = ends - padded
    n_tiles = (2 * n_tok + N_EXPERTS * (MOE_TM - 1) + MOE_TM - 1) // MOE_TM
    tile_row = jnp.arange(n_tiles, dtype=I32) * MOE_TM
    tile_expert = jnp.minimum(jnp.sum(tile_row[:, None] >= ends[None, :], axis=1), N_EXPERTS - 1).astype(I32)
    tile_valid = (tile_row < ends[-1]).astype(I32)
    tile_first = ((tile_row == starts[tile_expert]) & (tile_valid == 1)).astype(I32)
    pos1 = starts[route[:, 0].astype(I32)] + route[:, 4].astype(I32)
    pos2 = starts[route[:, 1].astype(I32)] + route[:, 5].astype(I32)

    inv = _inverse_perm(pos1, pos2, n_tiles * MOE_TM)
    ysort = _expert_mlp(tile_expert, tile_valid, tile_first, inv, x1t, w_e_gate[0], w_e_up[0], w_e_down[0])
    per_row = lambda col: jnp.repeat(col, nc).reshape(-1, 1)
    w1r, w2r = per_row(route[:, 2]), per_row(route[:, 3])
    ln2g_t, ln2b_t = ln2_g[0].reshape(nc, LANES), ln2_b[0].reshape(nc, LANES)
    y_p = _combine_ln(pos1, pos2, x1t, w1r, w2r, ln2g_t, ln2b_t, ysort, nc, 0, t)
    y_s = _combine_ln(pos1, pos2, x1t, w1r, w2r, ln2g_t, ln2b_t, ysort, nc, t, ms)

    kv5 = lambda a, n: a.reshape(1, n[0], n[1], H_A, 2 * DH_A)
    return (y_p.reshape(1, t, d), y_s.reshape(nb, ts, d),
            kv5(pp["ka"], (1, t)), kv5(pp["va"], (1, t)), kv5(pp["kb"], (1, t)), kv5(pp["vb"], (1, t)),
            pp["ki"].reshape(1, 1, t, DH_I),
            kv5(ps["ka"], (nb, ts)), kv5(ps["va"], (nb, ts)), kv5(ps["kb"], (nb, ts)),
            kv5(ps["vb"], (nb, ts)), ps["ki"].reshape(1, nb, ts, DH_I))
```

```python
import functools
import math

import numpy as np
import jax
import jax.numpy as jnp
from jax import lax
from jax.experimental import pallas as pl
from jax.experimental.pallas import tpu as pltpu

F32 = jnp.float32
BF16 = jnp.bfloat16
I32 = jnp.int32

H_A = 8
DH_A = 64
H_B = 8
DH_B = 128
H_I = 16
DH_I = 64
TOPK_MAX = 256
NUM_BUCKETS = 32
MAX_DISTANCE = 128
N_GROUPS = 4
EXPERTS_PER_GROUP = 8
N_EXPERTS = N_GROUPS * EXPERTS_PER_GROUP
PAGE_SIZE = 128
DEPTH = 1
ALPHA = (2 * DEPTH) ** 0.25
LN_EPS = 1e-5
LAMBDA_INIT = 0.8 - 0.6 * math.exp(-0.3 * 0)

LANES = 128
SUBLANES = 8
NEG = -1e30
INT_MIN = -2 ** 31
KEY_NEG_INF = int(np.array([-np.inf], np.float32).view(np.int32)[0]) ^ 0x7FFFFFFF
V7X_VMEM_LIMIT = 48 * 1024 * 1024

PROJ_TM = 512
FLASH_T = 1024
FLASH_SPLIT = 2
VT_ROWS = LANES + 16
SEL_TQ = 256
SEL_CH = 256
POST_TM = 256
MOE_TM = 256
SROWS = 8
N_NEW = 4
ATT_PAGES = 8
SEL_PAGES = 16
DMA_UNROLL = 8


def _t5_thresholds():
    n = np.arange(0, MAX_DISTANCE + 1)
    max_exact = NUM_BUCKETS // 2
    nf = np.maximum(n, 1).astype(np.float32)
    large = max_exact + (np.log(nf / max_exact) / math.log(MAX_DISTANCE / max_exact)
                         * (NUM_BUCKETS - max_exact)).astype(np.int32)
    b = np.where(n < max_exact, n, np.minimum(large, NUM_BUCKETS - 1))
    assert np.all(np.diff(b) >= 0) and b[-1] == NUM_BUCKETS - 1
    return tuple(int(np.argmax(b >= j)) for j in range(1, NUM_BUCKETS))


T5_THRESH = _t5_thresholds()
T5_FAR = T5_THRESH[-1]


def _t5_bias(d, table):
    b = jnp.broadcast_to(table(0), d.shape).astype(F32)
    for j, t in enumerate(T5_THRESH, start=1):
        b = jnp.where(d >= t, table(j), b)
    return b


def _dot_nt(a, b):
    return lax.dot_general(a, b, (((1,), (1,)), ((), ())), preferred_element_type=F32)


def _sort_key(x):
    bits = lax.bitcast_convert_type(x, I32)
    return jnp.where(bits < 0, bits ^ 0x7FFFFFFF, bits)


def _params(sem):
    return pltpu.CompilerParams(dimension_semantics=sem, vmem_limit_bytes=V7X_VMEM_LIMIT)


def _proj_kernel(x_ref, w_ref, *rest, emit, n_extra):
    extra, out_refs = rest[:n_extra], rest[n_extra:]
    res = jnp.dot(x_ref[...], w_ref[...], preferred_element_type=F32)
    emit(res, out_refs, x_ref, *extra)


def _emit_f32(res, outs, x_ref):
    outs[0][...] = res


def _emit_bf16(res, outs, x_ref):
    outs[0][...] = res.astype(BF16)


def _emit_k(res, outs, x_ref):
    outs[0][...] = res
    for h in range(res.shape[1] // LANES):
        outs[1][h] = res[:, h * LANES:(h + 1) * LANES].astype(BF16)


def _emit_v(res, outs, x_ref, wt_ref):
    outs[0][...] = res
    res_t = _dot_nt(wt_ref[...], x_ref[...])
    for h in range(res.shape[1] // LANES):
        outs[1][h, 0:LANES] = res_t[h * LANES:(h + 1) * LANES, :].astype(BF16)
        outs[1][h, LANES:VT_ROWS] = jnp.ones((VT_ROWS - LANES, res_t.shape[1]), BF16)


def _emit_heads(res, outs, x_ref):
    for h in range(res.shape[1] // LANES):
        outs[0][h] = res[:, h * LANES:(h + 1) * LANES].astype(BF16)


def _emit_qa(res, outs, x_ref):
    lane = lax.broadcasted_iota(I32, (res.shape[0], LANES), 1)
    for h in range(res.shape[1] // LANES):
        blk = res[:, h * LANES:(h + 1) * LANES] * (DH_A ** -0.5)
        outs[0][h, 0] = jnp.where(lane < DH_A, blk, 0.0).astype(BF16)
        outs[0][h, 1] = jnp.where(lane >= DH_A, blk, 0.0).astype(BF16)


def _emit_small(res, outs, x_ref):
    outs[0][...] = res
    outs[1][...] = res[:, :DH_I]
    outs[2][...] = res[:, :DH_I].astype(BF16)


def _proj(x, w, emit, out_shapes, out_blocks, tok_axes, name, extra=()):
    m, k = x.shape
    tm = min(PROJ_TM, m)

    def spec(blk, tok_axis):
        nd = len(blk)
        return pl.BlockSpec(blk, lambda i: tuple(i if a == tok_axis else 0 for a in range(nd)))

    full = lambda a: pl.BlockSpec(a.shape, lambda i: (0,) * a.ndim)
    return pl.pallas_call(
        functools.partial(_proj_kernel, emit=emit, n_extra=len(extra)),
        grid=(m // tm,),
        in_specs=[pl.BlockSpec((tm, k), lambda i: (i, 0)), full(w)] + [full(e) for e in extra],
        out_specs=[spec(b, a) for b, a in zip(out_blocks, tok_axes)],
        out_shape=out_shapes,
        compiler_params=_params(("arbitrary",)),
        name=name,
    )(x, w, *extra)


def _in_proj(x, w_bf, w_small_bf, with_attention_layouts):
    m, d = x.shape
    tm = min(PROJ_TM, m)
    hd = H_A * 2 * DH_A
    sds = jax.ShapeDtypeStruct
    cols = lambda j: w_bf[:, j * hd:(j + 1) * hd]
    out = {}
    out["qa"], = _proj(x, cols(0), _emit_qa, [sds((H_A, 2, m, LANES), BF16)], [(H_A, 2, tm, LANES)], [2], "proj_qa")
    for j, nm in ((1, "ka"), (4, "kb")):
        if with_attention_layouts:
            out[nm], out[nm + "_h"] = _proj(x, cols(j), _emit_k, [sds((m, hd), F32), sds((H_A, m, LANES), BF16)],
                                            [(tm, hd), (H_A, tm, LANES)], [0, 1], "proj_" + nm)
        else:
            out[nm], = _proj(x, cols(j), _emit_f32, [sds((m, hd), F32)], [(tm, hd)], [0], "proj_" + nm)
    for j, nm in ((2, "va"), (5, "vb")):
        if with_attention_layouts:
            out[nm], out[nm + "_t"] = _proj(x, cols(j), _emit_v, [sds((m, hd), F32), sds((H_A, VT_ROWS, m), BF16)],
                                            [(tm, hd), (H_A, VT_ROWS, tm)], [0, 2], "proj_" + nm,
                                            extra=(cols(j).T,))
        else:
            out[nm], = _proj(x, cols(j), _emit_f32, [sds((m, hd), F32)], [(tm, hd)], [0], "proj_" + nm)
    out["qb"], = _proj(x, cols(3), _emit_heads, [sds((H_B, m, LANES), BF16)], [(H_B, tm, LANES)], [1], "proj_qb")
    out["qi"], = _proj(x, cols(6), _emit_bf16, [sds((m, hd), BF16)], [(tm, hd)], [0], "proj_qi")
    g0 = 7 * hd + DH_I + H_I
    out["ga"], = _proj(x, w_bf[:, g0:g0 + d], _emit_f32, [sds((m, d), F32)], [(tm, d)], [0], "proj_ga")
    out["gb"], = _proj(x, w_bf[:, g0 + d:g0 + 2 * d], _emit_f32, [sds((m, d), F32)], [(tm, d)], [0], "proj_gb")
    out["small"], out["ki"], out["ki_bf"] = _proj(
        x, w_small_bf, _emit_small, [sds((m, LANES), F32), sds((m, DH_I), F32), sds((m, DH_I), BF16)],
        [(tm, LANES), (tm, DH_I), (tm, DH_I)], [0, 0, 0], "proj_small")
    return out


def _lambda_full(lam_ref):
    a = jnp.sum(lam_ref[0:1, :] * lam_ref[1:2, :], axis=1, keepdims=True)
    b = jnp.sum(lam_ref[2:3, :] * lam_ref[3:4, :], axis=1, keepdims=True)
    return jnp.exp(a) - jnp.exp(b) + LAMBDA_INIT


def _flash_kernel(qs_ref, ks_ref, tab_ref, lam_ref, sub_ref, q_ref, k_ref, vt_ref, *rest,
                  n_maps, tq, use_mask, scale, head_off):
    if use_mask:
        m_ref, o_ref, acc, m_s, bias_s = rest
    else:
        o_ref, acc, m_s, bias_s = rest
    h = pl.program_id(0)
    step = pl.program_id(1)
    qi = qs_ref[step]
    ki = ks_ref[step]
    cols = n_maps * tq
    hb = h + head_off
    table = lambda j: tab_ref[hb, j]

    @pl.when(step == 0)
    def _():
        r = lax.broadcasted_iota(I32, (tq, tq), 0)
        c = lax.broadcasted_iota(I32, (tq, tq), 1)
        d0 = c - r
        bias_s[0] = jnp.where(d0 >= 0, _t5_bias(d0, table), NEG)
        bias_s[1] = _t5_bias(d0 + tq, table)

    @pl.when(ki == 0)
    def _():
        m_s[...] = jnp.full(m_s.shape, NEG, F32)
        acc[...] = jnp.zeros(acc.shape, F32)

    def update(bias, uniform):
        q = q_ref[...].reshape(cols, LANES)
        kp = tq // FLASH_SPLIT
        parts = []
        for part in range(FLASH_SPLIT):
            rows = slice(part * kp, (part + 1) * kp)
            s = _dot_nt(k_ref[rows, :], q)
            if scale != 1.0:
                s = s * scale
            if not uniform:
                s = s + jnp.concatenate([bias[rows]] * n_maps, axis=1)
            if use_mask:
                s = s + m_ref[rows, :]
            parts.append(s)
        m_old = m_s[...]
        m_blk = jnp.max(parts[0], axis=0, keepdims=True)
        for s in parts[1:]:
            m_blk = jnp.maximum(m_blk, jnp.max(s, axis=0, keepdims=True))
        if uniform:
            m_blk = m_blk + bias
        m_new = jnp.maximum(m_old, m_blk)
        alpha = jnp.exp(m_old - m_new)
        shift = (m_new - bias) if uniform else m_new
        acc_new = alpha * acc[...]
        for part, s in enumerate(parts):
            p = jnp.exp(s - shift).astype(BF16)
            acc_new = acc_new + jnp.dot(vt_ref[:, part * kp:(part + 1) * kp], p, preferred_element_type=F32)
        acc[...] = acc_new
        m_s[...] = m_new

    @pl.when(qi - ki >= 2)
    def _():
        update(tab_ref[hb, NUM_BUCKETS - 1], True)

    @pl.when(qi - ki < 2)
    def _():
        update(bias_s[qi - ki], False)

    @pl.when(ki == qi)
    def _():
        o = acc[0:LANES, :] / acc[LANES:LANES + 1, :]
        if n_maps == 2:
            o = o[:, :tq] - _lambda_full(lam_ref) * o[:, tq:]
            o = o * lax.rsqrt(jnp.mean(o * o, axis=0, keepdims=True) + LN_EPS) * sub_ref[...] * (1.0 - LAMBDA_INIT)
        o_ref[...] = o.T.astype(BF16)


def _flash(q, k, vt, mask_t, tab, lam4, subw_col, *, n_maps, scale, head_off, name):
    nh, t = k.shape[0], k.shape[1]
    tq = min(FLASH_T, t)
    nq = t // tq
    pairs = [(a, b) for a in range(nq) for b in range(a + 1)]
    qs = jnp.asarray([p[0] for p in pairs], I32)
    ks = jnp.asarray([p[1] for p in pairs], I32)
    cols = n_maps * tq
    smem = pl.BlockSpec(memory_space=pltpu.SMEM)
    if n_maps == 2:
        q_spec = pl.BlockSpec((None, 2, tq, LANES), lambda h, s, qs, ks: (h, 0, qs[s], 0))
    else:
        q_spec = pl.BlockSpec((None, tq, LANES), lambda h, s, qs, ks: (h, qs[s], 0))
    in_specs = [smem, pl.BlockSpec((4, DH_A), lambda h, s, qs, ks: (0, 0)),
                pl.BlockSpec((LANES, 1), lambda h, s, qs, ks: (0, 0)), q_spec,
                pl.BlockSpec((None, tq, LANES), lambda h, s, qs, ks: (h, ks[s], 0)),
                pl.BlockSpec((None, VT_ROWS, tq), lambda h, s, qs, ks: (h, 0, ks[s]))]
    args = [tab, lam4, subw_col, q, k, vt]
    if mask_t is not None:
        in_specs.append(pl.BlockSpec((tq, tq), lambda h, s, qs, ks: (ks[s], qs[s])))
        args.append(mask_t)
    grid_spec = pltpu.PrefetchScalarGridSpec(
        num_scalar_prefetch=2,
        grid=(nh, len(pairs)),
        in_specs=in_specs,
        out_specs=pl.BlockSpec((tq, LANES), lambda h, s, qs, ks: (qs[s], h)),
        scratch_shapes=[pltpu.VMEM((VT_ROWS, cols), F32), pltpu.VMEM((1, cols), F32),
                        pltpu.VMEM((2, tq, tq), F32)],
    )
    return pl.pallas_call(
        functools.partial(_flash_kernel, n_maps=n_maps, tq=tq, use_mask=mask_t is not None, scale=scale,
                          head_off=head_off),
        grid_spec=grid_spec,
        out_shape=jax.ShapeDtypeStruct((t, nh * LANES), BF16),
        compiler_params=_params(("arbitrary", "arbitrary")),
        name=name,
    )(qs, ks, *args)


def _kth_largest_key(count_ge, shape, k_top):
    cand0 = jnp.zeros(shape, I32)
    res = jnp.where(count_ge(cand0) >= k_top, cand0, jnp.full(shape, INT_MIN, I32))

    def bit_body(b, res):
        cand = res + jnp.left_shift(jnp.int32(1), jnp.int32(30) - b)
        return jnp.where(count_ge(cand) >= k_top, cand, res)

    return lax.fori_loop(0, 31, bit_body, res)


def _index_select_kernel(qh_ref, wt_ref, ki_ref, o_ref, keys_s, *, tq, ch, n_ch, k_top):
    i = pl.program_id(0)
    q0 = i * tq
    n_valid = (q0 + tq + ch - 1) // ch
    wt = wt_ref[...] * (DH_I ** -0.5 * H_I ** -0.5)
    krow = lax.broadcasted_iota(I32, (ch, tq), 0)
    qpos = lax.broadcasted_iota(I32, (ch, tq), 1) + q0

    def score_chunk(c, _):
        k0 = pl.multiple_of(c * ch, ch)
        kc = ki_ref[pl.ds(k0, ch), :]
        acc = jnp.zeros((ch, tq), F32)
        for h in range(H_I):
            acc = acc + wt[h:h + 1, :] * jnp.maximum(_dot_nt(kc, qh_ref[h]), 0.0)
        val = jnp.where(krow + k0 <= qpos, acc, -jnp.inf)
        keys_s[pl.ds(k0, ch), :] = _sort_key(val)
        return 0

    lax.fori_loop(0, n_valid, score_chunk, 0)

    def count_ge(cand):
        def chunk(c, a):
            blk = keys_s[pl.ds(pl.multiple_of(c * ch, ch), ch), :]
            hit = (blk >= cand[0:1, :]).astype(I32)
            return a + jnp.sum(hit.reshape(ch // SUBLANES, SUBLANES, tq), axis=0)
        a = lax.fori_loop(0, n_valid, chunk, jnp.zeros((SUBLANES, tq), I32))
        return jnp.sum(a.astype(F32), axis=0, keepdims=True)

    thr = _kth_largest_key(count_ge, (SUBLANES, tq), k_top)[0:1, :]

    def write_chunk(c, _):
        k0 = pl.multiple_of(c * ch, ch)
        key = keys_s[pl.ds(k0, ch), :]
        o_ref[pl.ds(k0, ch), :] = jnp.where((key >= thr) & (key > KEY_NEG_INF), 0.0, NEG)
        return 0

    def fill_chunk(c, _):
        o_ref[pl.ds(pl.multiple_of(c * ch, ch), ch), :] = jnp.full((ch, tq), NEG, F32)
        return 0

    lax.fori_loop(0, n_valid, write_chunk, 0)
    lax.fori_loop(n_valid, n_ch, fill_chunk, 0)


def _index_select(qh, wt, ki_bf, k_top):
    t = ki_bf.shape[0]
    tq = min(SEL_TQ, t)
    ch = min(SEL_CH, t)
    return pl.pallas_call(
        functools.partial(_index_select_kernel, tq=tq, ch=ch, n_ch=t // ch, k_top=k_top),
        grid=(t // tq,),
        in_specs=[pl.BlockSpec((H_I, tq, DH_I), lambda i: (0, i, 0)),
                  pl.BlockSpec((H_I, tq), lambda i: (0, i)),
                  pl.BlockSpec((t, DH_I), lambda i: (0, 0))],
        out_specs=pl.BlockSpec((t, tq), lambda i: (0, i)),
        out_shape=jax.ShapeDtypeStruct((t, t), F32),
        scratch_shapes=[pltpu.VMEM((t, tq), I32)],
        compiler_params=_params(("arbitrary",)),
        name="index_select",
    )(qh, wt, ki_bf)


def _sample_select_kernel(pt_ref, q_ref, wb_ref, *rest, n_pages, n_out, k_top, n_rep):
    kc_refs, (kn_ref, o_ref, keys_s) = rest[:SEL_PAGES], rest[SEL_PAGES:]
    j = pl.program_id(1)
    n_steps = n_pages // SEL_PAGES
    row = lax.broadcasted_iota(I32, (SROWS, LANES), 0)
    lane = lax.broadcasted_iota(I32, (SROWS, LANES), 1)
    tok = row % N_NEW

    def score(k_f32):
        s = jnp.dot(q_ref[...], k_f32.astype(BF16), preferred_element_type=F32)
        s = jnp.maximum(s, 0.0) * wb_ref[...]
        return jnp.sum(s.reshape(H_I, SROWS, LANES), axis=0)

    @pl.when(j < n_steps)
    def _():
        for g in range(SEL_PAGES):
            keys_s[j * SEL_PAGES + g] = _sort_key(score(kc_refs[g][...]))

    @pl.when(j == n_steps)
    def _():
        val = jnp.where((lane <= tok) & (lane < N_NEW), score(kn_ref[...]), -jnp.inf)
        keys_s[n_pages] = _sort_key(val)
        keys_s[n_pages + 1] = jnp.full((SROWS, LANES), INT_MIN, I32)
        keys = keys_s[...]

        def count_ge(cand):
            hit = (keys >= cand[None]).astype(I32)
            return jnp.sum(jnp.sum(hit, axis=0).astype(F32), axis=1, keepdims=True)

        thr = _kth_largest_key(count_ge, (SROWS, LANES), k_top)
        sel = jnp.where((keys >= thr[None]) & (keys > KEY_NEG_INF), 1.0, 0.0)
        sel = sel.reshape((n_pages + 2) * SROWS, LANES).astype(BF16)
        pos = lax.broadcasted_iota(I32, (LANES, LANES * n_rep), 0)
        rep_lane = lax.broadcasted_iota(I32, (LANES, LANES * n_rep), 1)
        expand = (jnp.right_shift(rep_lane, int(math.log2(n_rep))) == pos).astype(BF16)
        wide = jnp.dot(sel, expand, preferred_element_type=F32)
        o_ref[0:n_pages + 2] = jnp.where(wide > 0.5, 0.0, NEG).reshape(n_pages + 2, SROWS, LANES * n_rep)
        for c in range(n_pages + 2, n_out):
            o_ref[c] = jnp.full((SROWS, LANES * n_rep), NEG, F32)


def _sample_select(page_table, qh, wb, cache_k_idx, ki_new, k_top, n_out, n_rep):
    nb, n_pages = page_table.shape
    assert n_pages % SEL_PAGES == 0 and n_out >= n_pages + 2 and n_rep & (n_rep - 1) == 0

    def page_spec(g):
        return pl.BlockSpec((None, DH_I, PAGE_SIZE),
                            lambda b, j, pt: (pt[b, jnp.minimum(j * SEL_PAGES + g, n_pages - 1)], 0, 0))

    grid_spec = pltpu.PrefetchScalarGridSpec(
        num_scalar_prefetch=1,
        grid=(nb, n_pages // SEL_PAGES + 1),
        in_specs=[pl.BlockSpec((None, H_I * SROWS, DH_I), lambda b, j, pt: (b, 0, 0)),
                  pl.BlockSpec((None, H_I * SROWS, LANES), lambda b, j, pt: (b, 0, 0))]
                 + [page_spec(g) for g in range(SEL_PAGES)]
                 + [pl.BlockSpec((None, DH_I, PAGE_SIZE), lambda b, j, pt: (b, 0, 0))],
        out_specs=pl.BlockSpec((None, n_out, SROWS, LANES * n_rep), lambda b, j, pt: (b, 0, 0, 0)),
        scratch_shapes=[pltpu.VMEM((n_pages + 2, SROWS, LANES), I32)],
    )
    return pl.pallas_call(
        functools.partial(_sample_select_kernel, n_pages=n_pages, n_out=n_out, k_top=k_top, n_rep=n_rep),
        grid_spec=grid_spec,
        out_shape=jax.ShapeDtypeStruct((nb, n_out, SROWS, LANES * n_rep), F32),
        compiler_params=_params(("arbitrary", "arbitrary")),
        name="sample_select",
    )(page_table, qh, wb, *([cache_k_idx] * SEL_PAGES), ki_new)


def _sample_attn_kernel(pt_ref, tabr_ref, lam_ref, sub_ref, q_ref, *rest,
                        n_maps, use_mask, scale, n_pages, n_heads):
    g_n = ATT_PAGES
    kc_refs, vc_refs = rest[:g_n], rest[g_n:2 * g_n]
    rest = rest[2 * g_n:]
    if use_mask:
        kn_ref, vn_ref, m_ref, o_ref, acc, m_s, l_s, far_s = rest
    else:
        kn_ref, vn_ref, o_ref, acc, m_s, l_s, far_s = rest
    j = pl.program_id(1)
    n_steps = n_pages // g_n
    past_len = n_pages * PAGE_SIZE
    rows = n_heads * SROWS
    wide = n_heads * PAGE_SIZE
    head_shift = int(math.log2(n_heads))
    table = lambda b: tabr_ref[:, b:b + 1]

    def geometry():
        row = lax.broadcasted_iota(I32, (rows, wide), 0)
        lane = lax.broadcasted_iota(I32, (rows, wide), 1)
        own = (lane & (n_heads - 1)) == jnp.right_shift(row, int(math.log2(SROWS)))
        return row % N_NEW, jnp.right_shift(lane, head_shift), own

    @pl.when(j == 0)
    def _():
        m_s[...] = jnp.full(m_s.shape, NEG, F32)
        l_s[...] = jnp.zeros(l_s.shape, F32)
        acc[...] = jnp.zeros(acc.shape, F32)
        far_s[...] = jnp.where(geometry()[2], table(NUM_BUCKETS - 1), NEG)

    def update(k_refs, v_refs, biases, page_masks):
        parts = []
        for k_ref, bias, page_mask in zip(k_refs, biases, page_masks):
            s = _dot_nt(q_ref[...], k_ref[...].astype(BF16))
            if scale != 1.0:
                s = s * scale
            s = s + bias
            if page_mask is not None:
                s = s + jnp.concatenate([page_mask] * n_heads, axis=0)
            parts.append(s)
        m_old = m_s[...]
        m_blk = jnp.max(parts[0], axis=1, keepdims=True)
        for s in parts[1:]:
            m_blk = jnp.maximum(m_blk, jnp.max(s, axis=1, keepdims=True))
        m_new = jnp.maximum(m_old, m_blk)
        alpha = jnp.exp(m_old - m_new)
        l_new = alpha * l_s[...]
        acc_new = alpha * acc[...]
        for s, v_ref in zip(parts, v_refs):
            pr = jnp.exp(s - m_new)
            l_new = l_new + jnp.sum(pr, axis=1, keepdims=True)
            acc_new = acc_new + jnp.dot(pr.astype(BF16), v_ref[...].astype(BF16), preferred_element_type=F32)
        l_s[...] = l_new
        acc[...] = acc_new
        m_s[...] = m_new

    first_near = (past_len - T5_FAR - PAGE_SIZE + 1) // PAGE_SIZE + 1
    assert first_near // g_n == n_steps - 1, "only the last cache step may hold near pages"
    step_masks = lambda: [m_ref[g] if use_mask else None for g in range(g_n)]

    @pl.when(j < n_steps - 1)
    def _():
        far = far_s[...]
        update(kc_refs, vc_refs, [far] * g_n, step_masks())

    @pl.when(j == n_steps - 1)
    def _():
        tok, pos, own = geometry()
        biases = []
        for g in range(g_n):
            page = (n_steps - 1) * g_n + g
            if page < first_near:
                biases.append(far_s[...])
            else:
                dist = (past_len + tok) - (page * PAGE_SIZE + pos)
                biases.append(jnp.where(own, _t5_bias(dist, table), NEG))
        update(kc_refs, vc_refs, biases, step_masks())

    @pl.when(j == n_steps)
    def _():
        tok, pos, own = geometry()
        dist = tok - pos
        bias = jnp.where(own & (dist >= 0) & (pos < N_NEW), _t5_bias(dist, table), NEG)
        update([kn_ref], [vn_ref], [bias], [m_ref[0] if use_mask else None])
        o = acc[...] / l_s[...]
        for h in range(n_heads):
            oh = o[h * SROWS:(h + 1) * SROWS]
            if n_maps == 2:
                oh = oh - _lambda_full(lam_ref) * pltpu.roll(oh, N_NEW, 0)
                oh = oh * lax.rsqrt(jnp.mean(oh * oh, axis=1, keepdims=True) + LN_EPS) * sub_ref[...] \
                    * (1.0 - LAMBDA_INIT)
            o_ref[:, h * LANES:(h + 1) * LANES] = oh


def _sample_attn(page_table, tab_rows, lam4, subw, q, cache_k, cache_v, k_new, v_new, mask, *, n_maps, scale, name):
    nb, n_pages = page_table.shape
    assert n_pages % ATT_PAGES == 0
    nh = q.shape[1] // SROWS
    assert nh & (nh - 1) == 0
    slab = PAGE_SIZE * nh

    def page_spec(g):
        return pl.BlockSpec((None, slab, LANES),
                            lambda b, j, pt: (pt[b, jnp.minimum(j * ATT_PAGES + g, n_pages - 1)], 0, 0))

    new_spec = pl.BlockSpec((None, slab, LANES), lambda b, j, pt: (b, 0, 0))
    in_specs = [pl.BlockSpec(tab_rows.shape, lambda b, j, pt: (0, 0)),
                pl.BlockSpec((4, DH_A), lambda b, j, pt: (0, 0)),
                pl.BlockSpec((1, LANES), lambda b, j, pt: (0, 0)),
                pl.BlockSpec((None, nh * SROWS, LANES), lambda b, j, pt: (b, 0, 0))]
    in_specs += [page_spec(g) for g in range(ATT_PAGES)] * 2 + [new_spec, new_spec]
    args = [tab_rows, lam4, subw, q] + [cache_k] * ATT_PAGES + [cache_v] * ATT_PAGES + [k_new, v_new]
    if mask is not None:
        in_specs.append(pl.BlockSpec((None, ATT_PAGES, SROWS, slab), lambda b, j, pt: (b, j, 0, 0)))
        args.append(mask)
    grid_spec = pltpu.PrefetchScalarGridSpec(
        num_scalar_prefetch=1,
        grid=(nb, n_pages // ATT_PAGES + 1),
        in_specs=in_specs,
        out_specs=pl.BlockSpec((None, SROWS, nh * LANES), lambda b, j, pt: (b, 0, 0)),
        scratch_shapes=[pltpu.VMEM((nh * SROWS, LANES), F32), pltpu.VMEM((nh * SROWS, 1), F32),
                        pltpu.VMEM((nh * SROWS, 1), F32), pltpu.VMEM((nh * SROWS, slab), F32)],
    )
    return pl.pallas_call(
        functools.partial(_sample_attn_kernel, n_maps=n_maps, use_mask=mask is not None, scale=scale,
                          n_pages=n_pages, n_heads=nh),
        grid_spec=grid_spec,
        out_shape=jax.ShapeDtypeStruct((nb, SROWS, nh * LANES), F32),
        compiler_params=_params(("arbitrary", "arbitrary")),
        name=name,
    )(page_table, *args)


def _gate_merge_kernel(oa_ref, ob_ref, ga_ref, gb_ref, wa_ref, wb_ref, o_ref):
    a = jnp.dot(oa_ref[...], wa_ref[...], preferred_element_type=F32)
    b = jnp.dot(ob_ref[...], wb_ref[...], preferred_element_type=F32)
    o_ref[...] = (jax.nn.sigmoid(ga_ref[...]) * a + jax.nn.sigmoid(gb_ref[...]) * b).astype(BF16)


def _gate_merge(oa, ob, ga, gb, wa, wb):
    m, d = ga.shape
    tm = min(POST_TM, m)
    row = lambda w: pl.BlockSpec((tm, w), lambda i: (i, 0))
    full = lambda a: pl.BlockSpec(a.shape, lambda i: (0, 0))
    return pl.pallas_call(
        _gate_merge_kernel,
        grid=(m // tm,),
        in_specs=[row(oa.shape[1]), row(ob.shape[1]), row(d), row(d), full(wa), full(wb)],
        out_specs=row(d),
        out_shape=jax.ShapeDtypeStruct((m, d), BF16),
        compiler_params=_params(("arbitrary",)),
        name="gate_merge",
    )(oa, ob, ga, gb, wa, wb)


def _store_token_major(ref, x):
    tm, d = x.shape
    nc = d // LANES
    for c in range(nc):
        ref[pl.ds(c, tm, stride=nc), :] = x[:, c * LANES:(c + 1) * LANES]


def _load_token_major(ref, start, tm, nc, dtype):
    return jnp.concatenate([ref[pl.ds(start + c, tm, stride=nc), :].astype(dtype) for c in range(nc)], axis=1)


def _layer_norm(x, g_ref, b_ref):
    mu = jnp.mean(x, axis=1, keepdims=True)
    xc = x - mu
    var = jnp.mean(xc * xc, axis=1, keepdims=True)
    return xc * lax.rsqrt(var + LN_EPS) * g_ref[...] + b_ref[...]


def _out_ln_route_kernel(mg_ref, x_ref, wo_ref, g_ref, b_ref, wr_ref, br_ref, cin_ref, *rest, tm):
    x1t_ref, route_ref, cnt_ref, carry = rest[-4:]
    i = pl.program_id(0)

    @pl.when(i == 0)
    def _():
        carry[...] = cin_ref[...]

    mix = jnp.dot(mg_ref[...], wo_ref[...], preferred_element_type=F32)
    x1 = _layer_norm(ALPHA * x_ref[...] + mix, g_ref, b_ref)
    _store_token_major(x1t_ref, x1)

    z = jnp.dot(x1, wr_ref[...], preferred_element_type=F32, precision=lax.Precision.HIGHEST) + br_ref[...]
    lane = lax.broadcasted_iota(I32, (tm, LANES), 1).astype(F32)
    ninf = -jnp.inf
    big = jnp.float32(2 ** 30)
    gl = jnp.where(lane < N_GROUPS, z, ninf)
    gmax = jnp.max(gl, axis=1, keepdims=True)
    gsel = jnp.min(jnp.where(gl == gmax, lane, big), axis=1, keepdims=True)
    p_g = 1.0 / jnp.sum(jnp.exp(gl - gmax), axis=1, keepdims=True)
    e_lane = lane - N_GROUPS
    in_grp = (e_lane >= gsel * EXPERTS_PER_GROUP) & (e_lane < (gsel + 1) * EXPERTS_PER_GROUP)
    el = jnp.where(in_grp, z, ninf)
    v1 = jnp.max(el, axis=1, keepdims=True)
    i1 = jnp.min(jnp.where(el == v1, lane, big), axis=1, keepdims=True)
    el2 = jnp.where(lane == i1, ninf, el)
    v2 = jnp.max(el2, axis=1, keepdims=True)
    i2 = jnp.min(jnp.where(el2 == v2, lane, big), axis=1, keepdims=True)
    e2x = jnp.exp(v2 - v1)
    w1 = p_g / (1.0 + e2x)
    w2 = p_g * e2x / (1.0 + e2x)
    e1 = i1 - N_GROUPS
    e2 = i2 - N_GROUPS

    hot1 = lane == e1
    hot2 = lane == e2
    onehot = (hot1 | hot2).astype(BF16)
    r = lax.broadcasted_iota(I32, (tm, tm), 0)
    c = lax.broadcasted_iota(I32, (tm, tm), 1)
    tri = (r > c).astype(BF16)
    prefix = jnp.dot(tri, onehot, preferred_element_type=F32) + carry[...]
    rank1 = jnp.sum(jnp.where(hot1, prefix, 0.0), axis=1, keepdims=True)
    rank2 = jnp.sum(jnp.where(hot2, prefix, 0.0), axis=1, keepdims=True)
    carry[...] = carry[...] + jnp.sum(onehot.astype(F32), axis=0, keepdims=True)
    cnt_ref[...] = carry[...]

    route = jnp.zeros((tm, LANES), F32)
    for j, col in enumerate((e1, e2, w1, w2, rank1, rank2)):
        route = jnp.where(lane == j, col, route)
    route_ref[...] = route


def _out_ln_route(merged, x, wo, g, b, wr, br, carry_in, x1t_prev, tok_off, n_tok_total):
    m, d = x.shape
    tm = min(POST_TM, m)
    nc = d // LANES
    assert tok_off % tm == 0
    blk_off = tok_off // tm
    row = lambda w: pl.BlockSpec((tm, w), lambda i: (i, 0))
    full = lambda a: pl.BlockSpec(a.shape, lambda i: (0, 0))
    sds = jax.ShapeDtypeStruct
    in_specs = [row(d), row(d), full(wo), full(g), full(b), full(wr), full(br), full(carry_in)]
    args = [merged, x, wo, g, b, wr, br, carry_in]
    aliases = {}
    if x1t_prev is not None:
        in_specs.append(pl.BlockSpec(memory_space=pl.ANY))
        args.append(x1t_prev)
        aliases = {len(args) - 1: 0}
    return pl.pallas_call(
        functools.partial(_out_ln_route_kernel, tm=tm),
        grid=(m // tm,),
        in_specs=in_specs,
        out_specs=[pl.BlockSpec((tm * nc, LANES), lambda i: (i + blk_off, 0)), row(LANES),
                   pl.BlockSpec((1, LANES), lambda i: (0, 0))],
        out_shape=[sds((n_tok_total * nc, LANES), F32), sds((m, LANES), F32), sds((1, LANES), F32)],
        scratch_shapes=[pltpu.VMEM((1, LANES), F32)],
        input_output_aliases=aliases,
        compiler_params=_params(("arbitrary",)),
        name="out_ln_route",
    )(*args)


def _inverse_perm_kernel(starts_ref, ea_ref, eb_ref, ra_ref, rb_ref, pos1_ref, pos2_ref, inv_ref, *, n_tok, n_slots):
    def zero(s, _):
        inv_ref[s] = 0
        return 0

    def put(t, _):
        p1 = starts_ref[ea_ref[t]] + ra_ref[t]
        p2 = starts_ref[eb_ref[t]] + rb_ref[t]
        pos1_ref[t] = p1
        pos2_ref[t] = p2
        inv_ref[p1] = t
        inv_ref[p2] = t
        return 0

    lax.fori_loop(0, n_slots, zero, 0, unroll=DMA_UNROLL)
    lax.fori_loop(0, n_tok, put, 0, unroll=DMA_UNROLL)


def _inverse_perm(starts, e1, e2, rank1, rank2, n_slots):
    smem = pl.BlockSpec(memory_space=pltpu.SMEM)
    n_tok = e1.shape[0]
    sds = jax.ShapeDtypeStruct
    return pl.pallas_call(
        functools.partial(_inverse_perm_kernel, n_tok=n_tok, n_slots=n_slots),
        in_specs=[smem] * 5,
        out_specs=[smem] * 3,
        out_shape=[sds((n_tok,), I32), sds((n_tok,), I32), sds((n_slots,), I32)],
        name="inverse_perm",
    )(starts, e1, e2, rank1, rank2)


def _expert_mlp_kernel(te_ref, tv_ref, tf_ref, inv_ref, x_hbm, wg_ref, wu_ref, wd_ref, o_ref,
                       wg_s, wu_s, wd_s, xbuf, sem, *, n_tiles, nc):
    i = pl.program_id(0)
    slab = MOE_TM * nc

    def row_copy(tile, r):
        slot = tile % 2
        src = x_hbm.at[pl.ds(pl.multiple_of(inv_ref[tile * MOE_TM + r] * nc, nc), nc)]
        dst = xbuf.at[pl.ds(pl.multiple_of(slot * slab + r * nc, nc), nc)]
        return pltpu.make_async_copy(src, dst, sem.at[slot])

    def start_gather(tile):
        lax.fori_loop(0, MOE_TM, lambda r, _: (row_copy(tile, r).start(), 0)[1], 0, unroll=DMA_UNROLL)

    def wait_gather(tile):
        lax.fori_loop(0, MOE_TM, lambda r, _: (row_copy(tile, r).wait(), 0)[1], 0, unroll=DMA_UNROLL)

    @pl.when((i == 0) & (tv_ref[0] == 1))
    def _():
        start_gather(0)

    @pl.when((i + 1 < n_tiles) & (tv_ref[jnp.minimum(i + 1, n_tiles - 1)] == 1))
    def _():
        start_gather(i + 1)

    @pl.when(tf_ref[i] == 1)
    def _():
        wg_s[...] = wg_ref[...].astype(BF16)
        wu_s[...] = wu_ref[...].astype(BF16)
        wd_s[...] = wd_ref[...].astype(BF16)

    @pl.when(tv_ref[i] == 1)
    def _():
        wait_gather(i)
        x = _load_token_major(xbuf, (i % 2) * slab, MOE_TM, nc, BF16)
        g = jnp.dot(x, wg_s[...], preferred_element_type=F32)
        u = jnp.dot(x, wu_s[...], preferred_element_type=F32)
        hid = (jax.nn.silu(g) * u).astype(BF16)
        _store_token_major(o_ref, jnp.dot(hid, wd_s[...], preferred_element_type=F32))

    @pl.when(tv_ref[i] == 0)
    def _():
        o_ref[...] = jnp.zeros(o_ref.shape, F32)


def _expert_mlp(tile_expert, tile_valid, tile_first, inv, x1t, w_gate, w_up, w_down):
    d, ff = w_gate.shape[1], w_gate.shape[2]
    nc = d // LANES
    n_tiles = inv.shape[0] // MOE_TM
    grid_spec = pltpu.PrefetchScalarGridSpec(
        num_scalar_prefetch=4, grid=(n_tiles,),
        in_specs=[pl.BlockSpec(memory_space=pl.ANY),
                  pl.BlockSpec((None, d, ff), lambda i, te, tv, tf, inv: (te[i], 0, 0)),
                  pl.BlockSpec((None, d, ff), lambda i, te, tv, tf, inv: (te[i], 0, 0)),
                  pl.BlockSpec((None, ff, d), lambda i, te, tv, tf, inv: (te[i], 0, 0))],
        out_specs=pl.BlockSpec((MOE_TM * nc, LANES), lambda i, te, tv, tf, inv: (i, 0)),
        scratch_shapes=[pltpu.VMEM((d, ff), BF16), pltpu.VMEM((d, ff), BF16), pltpu.VMEM((ff, d), BF16),
                        pltpu.VMEM((2 * MOE_TM * nc, LANES), F32), pltpu.SemaphoreType.DMA((2,))])
    return pl.pallas_call(
        functools.partial(_expert_mlp_kernel, n_tiles=n_tiles, nc=nc),
        grid_spec=grid_spec,
        out_shape=jax.ShapeDtypeStruct((inv.shape[0] * nc, LANES), F32),
        compiler_params=_params(("arbitrary",)),
        name="expert_mlp",
    )(tile_expert, tile_valid, tile_first, inv, x1t, w_gate, w_up, w_down)


def _combine_ln_kernel(pos1_ref, pos2_ref, x1t_ref, route_ref, g_ref, b_ref, ys_hbm, o_ref, g1, g2, sem,
                       *, tm, nc, tok_off):
    base = tok_off + pl.program_id(0) * tm

    def row_copy(r, pos_ref, dst, slot):
        src = ys_hbm.at[pl.ds(pl.multiple_of(pos_ref[base + r] * nc, nc), nc)]
        return pltpu.make_async_copy(src, dst.at[pl.ds(pl.multiple_of(r * nc, nc), nc)], sem.at[slot])

    def start(r, _):
        row_copy(r, pos1_ref, g1, 0).start()
        row_copy(r, pos2_ref, g2, 1).start()
        return 0

    def wait(r, _):
        row_copy(r, pos1_ref, g1, 0).wait()
        row_copy(r, pos2_ref, g2, 1).wait()
        return 0

    lax.fori_loop(0, tm, start, 0, unroll=DMA_UNROLL)
    lax.fori_loop(0, tm, wait, 0, unroll=DMA_UNROLL)
    w1 = route_ref[:, 2:3]
    w2 = route_ref[:, 3:4]
    f = w1 * _load_token_major(g1, 0, tm, nc, F32) + w2 * _load_token_major(g2, 0, tm, nc, F32)
    o_ref[...] = _layer_norm(ALPHA * _load_token_major(x1t_ref, 0, tm, nc, F32) + f, g_ref, b_ref)


def _combine_ln(pos1, pos2, x1t, route, g, b, ys, nc, tok_off):
    m = route.shape[0]
    d = nc * LANES
    tm = min(POST_TM, m)
    assert tok_off % tm == 0
    blk_off = tok_off // tm
    full = lambda a: pl.BlockSpec(a.shape, lambda i, p1, p2: (0, 0))
    grid_spec = pltpu.PrefetchScalarGridSpec(
        num_scalar_prefetch=2, grid=(m // tm,),
        in_specs=[pl.BlockSpec((tm * nc, LANES), lambda i, p1, p2: (i + blk_off, 0)),
                  pl.BlockSpec((tm, LANES), lambda i, p1, p2: (i, 0)),
                  full(g), full(b), pl.BlockSpec(memory_space=pl.ANY)],
        out_specs=pl.BlockSpec((tm, d), lambda i, p1, p2: (i, 0)),
        scratch_shapes=[pltpu.VMEM((tm * nc, LANES), F32), pltpu.VMEM((tm * nc, LANES), F32),
                        pltpu.SemaphoreType.DMA((2,))])
    return pl.pallas_call(
        functools.partial(_combine_ln_kernel, tm=tm, nc=nc, tok_off=tok_off),
        grid_spec=grid_spec,
        out_shape=jax.ShapeDtypeStruct((m, d), F32),
        compiler_params=_params(("arbitrary",)),
        name="combine_ln",
    )(pos1, pos2, x1t, route, g, b, ys)


def kernel(x_prompt, x_sample, cache_k_a, cache_v_a, cache_k_b, cache_v_b, cache_k_idx, page_table, w_in,
           lambda_q1, lambda_k1, lambda_q2, lambda_k2, subln_w, w_br_a, w_br_b, w_o, rel_bias, ln1_g, ln1_b,
           w_router_group, b_router_group, w_router_expert, b_router_expert, w_e_gate, w_e_up, w_e_down,
           ln2_g, ln2_b):
    assert w_in.shape[0] == DEPTH
    bp, t, d = x_prompt.shape
    assert bp == 1
    nb, ts, _ = x_sample.shape
    assert ts == N_NEW
    n_pages = page_table.shape[1]
    past_len = n_pages * PAGE_SIZE
    hd = H_A * 2 * DH_A
    n_phys = cache_k_a.shape[1]
    ms = nb * ts
    nc = d // LANES

    w = w_in[0]
    w_bf = w.astype(BF16)
    s0 = 7 * hd
    w_small = jnp.concatenate([w[:, s0:s0 + DH_I + H_I], jnp.zeros((d, LANES - DH_I - H_I), F32)], axis=1).astype(BF16)
    tab = rel_bias.T.astype(F32)
    lam4 = jnp.stack([lambda_q1[0], lambda_k1[0], lambda_q2[0], lambda_k2[0]]).astype(F32)
    subw = subln_w[0].reshape(1, 2 * DH_A).astype(F32)
    wa_bf = w_br_a[0].astype(BF16)
    wb_bf = w_br_b[0].astype(BF16)
    wo_bf = w_o[0].astype(BF16)
    w_route = jnp.concatenate([w_router_group[0], w_router_expert[0],
                               jnp.zeros((d, LANES - N_GROUPS - N_EXPERTS), F32)], axis=1)
    b_route = jnp.concatenate([b_router_group[0], b_router_expert[0],
                               jnp.zeros((LANES - N_GROUPS - N_EXPERTS,), F32)]).reshape(1, LANES)
    ln1g, ln1b = ln1_g[0].reshape(1, d), ln1_b[0].reshape(1, d)

    xp = x_prompt.reshape(t, d)
    xs = x_sample.reshape(ms, d)
    pp = _in_proj(xp.astype(BF16), w_bf, w_small, True)
    ps = _in_proj(xs.astype(BF16), w_bf, w_small, False)

    oa_p = _flash(pp["qa"], pp["ka_h"], pp["va_t"], None, tab, lam4, subw.reshape(2 * DH_A, 1),
                  n_maps=2, scale=1.0, head_off=0, name="flash_diff")
    qh_p = pp["qi"].reshape(t, H_I, DH_I).transpose(1, 0, 2)
    wt_p = pp["small"][:, DH_I:DH_I + H_I].T
    mask_p = _index_select(qh_p, wt_p, pp["ki_bf"], min(TOPK_MAX, t // 4))
    ob_p = _flash(pp["qb"], pp["kb_h"], pp["vb_t"], mask_p, tab, lam4, subw.reshape(2 * DH_A, 1),
                  n_maps=1, scale=DH_B ** -0.5, head_off=H_A, name="flash_dsa")

    def rows_to_batch(a):
        nh = a.shape[0]
        if a.ndim == 4:
            return a.reshape(nh, 2, nb, ts, LANES).transpose(2, 0, 1, 3, 4).reshape(nb, nh * 2 * ts, LANES)
        a = a.reshape(nh, nb, ts, LANES).transpose(1, 0, 2, 3)
        return jnp.pad(a, ((0, 0), (0, 0), (0, SROWS - ts), (0, 0))).reshape(nb, nh * SROWS, LANES)

    def new_slab(a):
        a = jnp.pad(a.reshape(nb, ts, H_A, LANES), ((0, 0), (0, PAGE_SIZE - ts), (0, 0), (0, 0)))
        return a.reshape(nb, PAGE_SIZE * H_A, LANES)

    def new_rows(a):
        return jnp.pad(a.reshape(nb, ts, a.shape[1]), ((0, 0), (0, PAGE_SIZE - ts), (0, 0)))

    page_view = lambda c: c.reshape(n_phys, PAGE_SIZE * H_A, LANES)
    tab_rows = lambda off: jnp.repeat(tab[off:off + H_A], SROWS, axis=0)
    oa_s = _sample_attn(page_table, tab_rows(0), lam4, subw, rows_to_batch(ps["qa"]),
                        page_view(cache_k_a), page_view(cache_v_a), new_slab(ps["ka"]), new_slab(ps["va"]), None,
                        n_maps=2, scale=1.0, name="sample_diff")
    qh_s = ps["qi"].reshape(nb, ts, H_I, DH_I).transpose(0, 2, 1, 3)
    qh_s = jnp.pad(qh_s, ((0, 0), (0, 0), (0, SROWS - ts), (0, 0))).reshape(nb, H_I * SROWS, DH_I)
    wi_s = ps["small"][:, DH_I:DH_I + H_I].reshape(nb, ts, H_I).transpose(0, 2, 1) * (DH_I ** -0.5 * H_I ** -0.5)
    wi_s = jnp.pad(wi_s, ((0, 0), (0, 0), (0, SROWS - ts))).reshape(nb, H_I * SROWS, 1)
    wb_s = jnp.broadcast_to(wi_s, (nb, H_I * SROWS, LANES))
    kidx_t = jnp.swapaxes(cache_k_idx.reshape(n_phys, PAGE_SIZE, DH_I), 1, 2)
    mask_s = _sample_select(page_table, qh_s, wb_s, kidx_t, jnp.swapaxes(new_rows(ps["ki"]), 1, 2),
                            min(TOPK_MAX, (past_len + ts) // 4), n_pages + ATT_PAGES, H_B)
    ob_s = _sample_attn(page_table, tab_rows(H_A), lam4, subw, rows_to_batch(ps["qb"]),
                        page_view(cache_k_b), page_view(cache_v_b), new_slab(ps["kb"]), new_slab(ps["vb"]), mask_s,
                        n_maps=1, scale=DH_B ** -0.5, name="sample_dsa")
    oa_s = oa_s[:, :ts].reshape(ms, hd).astype(BF16)
    ob_s = ob_s[:, :ts].reshape(ms, hd).astype(BF16)

    n_tok = t + ms
    mg_p = _gate_merge(oa_p, ob_p, pp["ga"], pp["gb"], wa_bf, wb_bf)
    mg_s = _gate_merge(oa_s, ob_s, ps["ga"], ps["gb"], wa_bf, wb_bf)
    x1t, route_p, cnt_p = _out_ln_route(mg_p, xp, wo_bf, ln1g, ln1b, w_route, b_route, jnp.zeros((1, LANES), F32),
                                        jnp.zeros((n_tok * nc, LANES), F32), 0, n_tok)
    x1t, route_s, cnt = _out_ln_route(mg_s, xs, wo_bf, ln1g, ln1b, w_route, b_route, cnt_p, x1t, t, n_tok)
    route = jnp.concatenate([route_p, route_s], axis=0)

    counts = cnt[0, :N_EXPERTS].astype(I32)
    padded = (counts + MOE_TM - 1) // MOE_TM * MOE_TM
    ends = jnp.cumsum(padded)
    starts = ends - padded
    n_tiles = (2 * n_tok + N_EXPERTS * (MOE_TM - 1) + MOE_TM - 1) // MOE_TM
    tile_row = jnp.arange(n_tiles, dtype=I32) * MOE_TM
    tile_expert = jnp.minimum(jnp.sum(tile_row[:, None] >= ends[None, :], axis=1), N_EXPERTS - 1).astype(I32)
    tile_valid = (tile_row < ends[-1]).astype(I32)
    tile_first = ((tile_row == starts[tile_expert]) & (tile_valid == 1)).astype(I32)
    ids = route[:, :SUBLANES].astype(I32)
    pos1, pos2, inv = _inverse_perm(starts.astype(I32), ids[:, 0], ids[:, 1], ids[:, 4], ids[:, 5],
                                    n_tiles * MOE_TM)
    ysort = _expert_mlp(tile_expert, tile_valid, tile_first, inv, x1t, w_e_gate[0], w_e_up[0], w_e_down[0])
    ln2g, ln2b = ln2_g[0].reshape(1, d), ln2_b[0].reshape(1, d)
    y_p = _combine_ln(pos1, pos2, x1t, route_p, ln2g, ln2b, ysort, nc, 0)
    y_s = _combine_ln(pos1, pos2, x1t, route_s, ln2g, ln2b, ysort, nc, t)

    kv5 = lambda a, n: a.reshape(1, n[0], n[1], H_A, 2 * DH_A)
    return (y_p.reshape(1, t, d), y_s.reshape(nb, ts, d),
            kv5(pp["ka"], (1, t)), kv5(pp["va"], (1, t)), kv5(pp["kb"], (1, t)), kv5(pp["vb"], (1, t)),
            pp["ki"].reshape(1, 1, t, DH_I),
            kv5(ps["ka"], (nb, ts)), kv5(ps["va"], (nb, ts)), kv5(ps["kb"], (nb, ts)),
            kv5(ps["vb"], (nb, ts)), ps["ki"].reshape(1, nb, ts, DH_I))
```

```python
import functools
import math

import numpy as np
import jax
import jax.numpy as jnp
from jax import lax
from jax.experimental import pallas as pl
from jax.experimental.pallas import tpu as pltpu

F32 = jnp.float32
BF16 = jnp.bfloat16
I32 = jnp.int32
I16 = jnp.int16

H_A = 8
DH_A = 64
H_B = 8
DH_B = 128
H_I = 16
DH_I = 64
TOPK_MAX = 256
NUM_BUCKETS = 32
MAX_DISTANCE = 128
N_GROUPS = 4
EXPERTS_PER_GROUP = 8
N_EXPERTS = N_GROUPS * EXPERTS_PER_GROUP
PAGE_SIZE = 128
DEPTH = 1
ALPHA = (2 * DEPTH) ** 0.25
LN_EPS = 1e-5
LAMBDA_INIT = 0.8 - 0.6 * math.exp(-0.3 * 0)

LANES = 128
SUBLANES = 8
PACK_ROWS = 16
NEG = -1e30
INT_MIN = -2 ** 31
KEY_NEG_INF = int(np.array([-np.inf], np.float32).view(np.int32)[0]) ^ 0x7FFFFFFF
V7X_VMEM_LIMIT = 48 * 1024 * 1024

PROJ_TM = 512
FLASH_T = 1024
FLASH_SPLIT = 2
VT_ROWS = LANES + 16
SEL_TQ = 256
SEL_CH = 256
POST_TM = 256
MOE_TM = 256
SROWS = 8
N_NEW = 4
ATT_PAGES = 8
SEL_PAGES = 16
DMA_UNROLL = 8


def _t5_thresholds():
    n = np.arange(0, MAX_DISTANCE + 1)
    max_exact = NUM_BUCKETS // 2
    nf = np.maximum(n, 1).astype(np.float32)
    large = max_exact + (np.log(nf / max_exact) / math.log(MAX_DISTANCE / max_exact)
                         * (NUM_BUCKETS - max_exact)).astype(np.int32)
    b = np.where(n < max_exact, n, np.minimum(large, NUM_BUCKETS - 1))
    assert np.all(np.diff(b) >= 0) and b[-1] == NUM_BUCKETS - 1
    return tuple(int(np.argmax(b >= j)) for j in range(1, NUM_BUCKETS))


T5_THRESH = _t5_thresholds()
T5_FAR = T5_THRESH[-1]


def _t5_bias(d, table):
    b = jnp.broadcast_to(table(0), d.shape).astype(F32)
    for j, t in enumerate(T5_THRESH, start=1):
        b = jnp.where(d >= t, table(j), b)
    return b


def _dot_nt(a, b):
    return lax.dot_general(a, b, (((1,), (1,)), ((), ())), preferred_element_type=F32)


def _sort_key(x):
    bits = lax.bitcast_convert_type(x, I32)
    return jnp.where(bits < 0, bits ^ 0x7FFFFFFF, bits)


def _params(sem):
    return pltpu.CompilerParams(dimension_semantics=sem, vmem_limit_bytes=V7X_VMEM_LIMIT)


def _proj_kernel(x_ref, w_ref, *rest, emit, n_extra):
    extra, out_refs = rest[:n_extra], rest[n_extra:]
    res = jnp.dot(x_ref[...], w_ref[...], preferred_element_type=F32)
    emit(res, out_refs, x_ref, *extra)


def _emit_f32(res, outs, x_ref):
    outs[0][...] = res


def _emit_bf16(res, outs, x_ref):
    outs[0][...] = res.astype(BF16)


def _emit_k(res, outs, x_ref):
    outs[0][...] = res
    for h in range(res.shape[1] // LANES):
        outs[1][h] = res[:, h * LANES:(h + 1) * LANES].astype(BF16)


def _emit_v(res, outs, x_ref, wt_ref):
    outs[0][...] = res
    res_t = _dot_nt(wt_ref[...], x_ref[...])
    for h in range(res.shape[1] // LANES):
        outs[1][h, 0:LANES] = res_t[h * LANES:(h + 1) * LANES, :].astype(BF16)
        outs[1][h, LANES:VT_ROWS] = jnp.ones((VT_ROWS - LANES, res_t.shape[1]), BF16)


def _emit_heads(res, outs, x_ref):
    for h in range(res.shape[1] // LANES):
        outs[0][h] = res[:, h * LANES:(h + 1) * LANES].astype(BF16)


def _emit_qa(res, outs, x_ref):
    lane = lax.broadcasted_iota(I32, (res.shape[0], LANES), 1)
    for h in range(res.shape[1] // LANES):
        blk = res[:, h * LANES:(h + 1) * LANES] * (DH_A ** -0.5)
        outs[0][h, 0] = jnp.where(lane < DH_A, blk, 0.0).astype(BF16)
        outs[0][h, 1] = jnp.where(lane >= DH_A, blk, 0.0).astype(BF16)


def _emit_small(res, outs, x_ref):
    outs[0][...] = res
    outs[1][...] = res[:, :DH_I]
    outs[2][...] = res[:, :DH_I].astype(BF16)


def _proj(x, w, emit, out_shapes, out_blocks, tok_axes, name, extra=()):
    m, k = x.shape
    tm = min(PROJ_TM, m)

    def spec(blk, tok_axis):
        nd = len(blk)
        return pl.BlockSpec(blk, lambda i: tuple(i if a == tok_axis else 0 for a in range(nd)))

    full = lambda a: pl.BlockSpec(a.shape, lambda i: (0,) * a.ndim)
    return pl.pallas_call(
        functools.partial(_proj_kernel, emit=emit, n_extra=len(extra)),
        grid=(m // tm,),
        in_specs=[pl.BlockSpec((tm, k), lambda i: (i, 0)), full(w)] + [full(e) for e in extra],
        out_specs=[spec(b, a) for b, a in zip(out_blocks, tok_axes)],
        out_shape=out_shapes,
        compiler_params=_params(("arbitrary",)),
        name=name,
    )(x, w, *extra)


def _in_proj(x, w_bf, w_small_bf, with_attention_layouts):
    m, d = x.shape
    tm = min(PROJ_TM, m)
    hd = H_A * 2 * DH_A
    sds = jax.ShapeDtypeStruct
    cols = lambda j: w_bf[:, j * hd:(j + 1) * hd]
    out = {}
    out["qa"], = _proj(x, cols(0), _emit_qa, [sds((H_A, 2, m, LANES), BF16)], [(H_A, 2, tm, LANES)], [2], "proj_qa")
    for j, nm in ((1, "ka"), (4, "kb")):
        if with_attention_layouts:
            out[nm], out[nm + "_h"] = _proj(x, cols(j), _emit_k, [sds((m, hd), F32), sds((H_A, m, LANES), BF16)],
                                            [(tm, hd), (H_A, tm, LANES)], [0, 1], "proj_" + nm)
        else:
            out[nm], = _proj(x, cols(j), _emit_f32, [sds((m, hd), F32)], [(tm, hd)], [0], "proj_" + nm)
    for j, nm in ((2, "va"), (5, "vb")):
        if with_attention_layouts:
            out[nm], out[nm + "_t"] = _proj(x, cols(j), _emit_v, [sds((m, hd), F32), sds((H_A, VT_ROWS, m), BF16)],
                                            [(tm, hd), (H_A, VT_ROWS, tm)], [0, 2], "proj_" + nm,
                                            extra=(cols(j).T,))
        else:
            out[nm], = _proj(x, cols(j), _emit_f32, [sds((m, hd), F32)], [(tm, hd)], [0], "proj_" + nm)
    out["qb"], = _proj(x, cols(3), _emit_heads, [sds((H_B, m, LANES), BF16)], [(H_B, tm, LANES)], [1], "proj_qb")
    out["qi"], = _proj(x, cols(6), _emit_bf16, [sds((m, hd), BF16)], [(tm, hd)], [0], "proj_qi")
    g0 = 7 * hd + DH_I + H_I
    out["ga"], = _proj(x, w_bf[:, g0:g0 + d], _emit_f32, [sds((m, d), F32)], [(tm, d)], [0], "proj_ga")
    out["gb"], = _proj(x, w_bf[:, g0 + d:g0 + 2 * d], _emit_f32, [sds((m, d), F32)], [(tm, d)], [0], "proj_gb")
    out["small"], out["ki"], out["ki_bf"] = _proj(
        x, w_small_bf, _emit_small, [sds((m, LANES), F32), sds((m, DH_I), F32), sds((m, DH_I), BF16)],
        [(tm, LANES), (tm, DH_I), (tm, DH_I)], [0, 0, 0], "proj_small")
    return out


def _lambda_full(lam_ref):
    a = jnp.sum(lam_ref[0:1, :] * lam_ref[1:2, :], axis=1, keepdims=True)
    b = jnp.sum(lam_ref[2:3, :] * lam_ref[3:4, :], axis=1, keepdims=True)
    return jnp.exp(a) - jnp.exp(b) + LAMBDA_INIT


def _flash_kernel(qs_ref, ks_ref, tab_ref, lam_ref, sub_ref, q_ref, k_ref, vt_ref, *rest,
                  n_maps, tq, use_mask, scale, head_off):
    if use_mask:
        m_ref, o_ref, acc, m_s, bias_s = rest
    else:
        o_ref, acc, m_s, bias_s = rest
    h = pl.program_id(0)
    step = pl.program_id(1)
    qi = qs_ref[step]
    ki = ks_ref[step]
    cols = n_maps * tq
    hb = h + head_off
    table = lambda j: tab_ref[hb, j]

    @pl.when(step == 0)
    def _():
        r = lax.broadcasted_iota(I32, (tq, tq), 0)
        c = lax.broadcasted_iota(I32, (tq, tq), 1)
        d0 = c - r
        bias_s[0] = jnp.where(d0 >= 0, _t5_bias(d0, table), NEG)
        bias_s[1] = _t5_bias(d0 + tq, table)

    @pl.when(ki == 0)
    def _():
        m_s[...] = jnp.full(m_s.shape, NEG, F32)
        acc[...] = jnp.zeros(acc.shape, F32)

    def update(bias, uniform):
        q = q_ref[...].reshape(cols, LANES)
        kp = tq // FLASH_SPLIT
        parts = []
        for part in range(FLASH_SPLIT):
            rows = slice(part * kp, (part + 1) * kp)
            s = _dot_nt(k_ref[rows, :], q)
            if scale != 1.0:
                s = s * scale
            if not uniform:
                s = s + jnp.concatenate([bias[rows]] * n_maps, axis=1)
            if use_mask:
                s = s + m_ref[rows, :]
            parts.append(s)
        m_old = m_s[...]
        m_blk = jnp.max(parts[0], axis=0, keepdims=True)
        for s in parts[1:]:
            m_blk = jnp.maximum(m_blk, jnp.max(s, axis=0, keepdims=True))
        if uniform:
            m_blk = m_blk + bias
        m_new = jnp.maximum(m_old, m_blk)
        alpha = jnp.exp(m_old - m_new)
        shift = (m_new - bias) if uniform else m_new
        acc_new = alpha * acc[...]
        for part, s in enumerate(parts):
            p = jnp.exp(s - shift).astype(BF16)
            acc_new = acc_new + jnp.dot(vt_ref[:, part * kp:(part + 1) * kp], p, preferred_element_type=F32)
        acc[...] = acc_new
        m_s[...] = m_new

    @pl.when(qi - ki >= 2)
    def _():
        update(tab_ref[hb, NUM_BUCKETS - 1], True)

    @pl.when(qi - ki < 2)
    def _():
        update(bias_s[qi - ki], False)

    @pl.when(ki == qi)
    def _():
        o = acc[0:LANES, :] / acc[LANES:LANES + 1, :]
        if n_maps == 2:
            o = o[:, :tq] - _lambda_full(lam_ref) * o[:, tq:]
            o = o * lax.rsqrt(jnp.mean(o * o, axis=0, keepdims=True) + LN_EPS) * sub_ref[...] * (1.0 - LAMBDA_INIT)
        o_ref[...] = o.T.astype(BF16)


def _flash(q, k, vt, mask_t, tab, lam4, subw_col, *, n_maps, scale, head_off, name):
    nh, t = k.shape[0], k.shape[1]
    tq = min(FLASH_T, t)
    nq = t // tq
    pairs = [(a, b) for a in range(nq) for b in range(a + 1)]
    qs = jnp.asarray([p[0] for p in pairs], I32)
    ks = jnp.asarray([p[1] for p in pairs], I32)
    cols = n_maps * tq
    smem = pl.BlockSpec(memory_space=pltpu.SMEM)
    if n_maps == 2:
        q_spec = pl.BlockSpec((None, 2, tq, LANES), lambda h, s, qs, ks: (h, 0, qs[s], 0))
    else:
        q_spec = pl.BlockSpec((None, tq, LANES), lambda h, s, qs, ks: (h, qs[s], 0))
    in_specs = [smem, pl.BlockSpec((4, DH_A), lambda h, s, qs, ks: (0, 0)),
                pl.BlockSpec((LANES, 1), lambda h, s, qs, ks: (0, 0)), q_spec,
                pl.BlockSpec((None, tq, LANES), lambda h, s, qs, ks: (h, ks[s], 0)),
                pl.BlockSpec((None, VT_ROWS, tq), lambda h, s, qs, ks: (h, 0, ks[s]))]
    args = [tab, lam4, subw_col, q, k, vt]
    if mask_t is not None:
        in_specs.append(pl.BlockSpec((tq, tq), lambda h, s, qs, ks: (ks[s], qs[s])))
        args.append(mask_t)
    grid_spec = pltpu.PrefetchScalarGridSpec(
        num_scalar_prefetch=2,
        grid=(nh, len(pairs)),
        in_specs=in_specs,
        out_specs=pl.BlockSpec((tq, LANES), lambda h, s, qs, ks: (qs[s], h)),
        scratch_shapes=[pltpu.VMEM((VT_ROWS, cols), F32), pltpu.VMEM((1, cols), F32),
                        pltpu.VMEM((2, tq, tq), F32)],
    )
    return pl.pallas_call(
        functools.partial(_flash_kernel, n_maps=n_maps, tq=tq, use_mask=mask_t is not None, scale=scale,
                          head_off=head_off),
        grid_spec=grid_spec,
        out_shape=jax.ShapeDtypeStruct((t, nh * LANES), BF16),
        compiler_params=_params(("arbitrary", "arbitrary")),
        name=name,
    )(qs, ks, *args)


def _kth_largest_key(count_ge, shape, k_top):
    cand0 = jnp.zeros(shape, I32)
    res = jnp.where(count_ge(cand0) >= k_top, cand0, jnp.full(shape, INT_MIN, I32))

    def bit_body(b, res):
        cand = res + jnp.left_shift(jnp.int32(1), jnp.int32(30) - b)
        return jnp.where(count_ge(cand) >= k_top, cand, res)

    return lax.fori_loop(0, 31, bit_body, res)


def _index_select_kernel(qh_ref, wt_ref, ki_ref, o_ref, keys_s, hi_s, lo_s, *, tq, ch, n_ch, k_top):
    i = pl.program_id(0)
    q0 = i * tq
    n_valid = (q0 + tq + ch - 1) // ch
    wt = wt_ref[...] * (DH_I ** -0.5 * H_I ** -0.5)
    krow = lax.broadcasted_iota(I32, (ch, tq), 0)
    qpos = lax.broadcasted_iota(I32, (ch, tq), 1) + q0

    def score_chunk(c, _):
        k0 = pl.multiple_of(c * ch, ch)
        kc = ki_ref[pl.ds(k0, ch), :]
        acc = jnp.zeros((ch, tq), F32)
        for h in range(H_I):
            acc = acc + wt[h:h + 1, :] * jnp.maximum(_dot_nt(kc, qh_ref[h]), 0.0)
        key = _sort_key(jnp.where(krow + k0 <= qpos, acc, -jnp.inf))
        keys_s[pl.ds(k0, ch), :] = key
        hi_s[pl.ds(k0, ch), :] = jnp.right_shift(key, 16).astype(I16)
        return 0

    lax.fori_loop(0, n_valid, score_chunk, 0)

    def count_ge16(half_s, cand):
        c16 = cand.astype(I16)

        def chunk(c, tot):
            blk = half_s[pl.ds(pl.multiple_of(c * ch, ch), ch), :]
            hit = jnp.where(blk >= c16, jnp.ones((), BF16), jnp.zeros((), BF16))
            parts = [hit[r * PACK_ROWS:(r + 1) * PACK_ROWS] for r in range(ch // PACK_ROWS)]
            while len(parts) > 1:
                parts = [parts[a] + parts[a + 1] for a in range(0, len(parts), 2)]
            return tot + parts[0].astype(F32)

        tot = lax.fori_loop(0, n_valid, chunk, jnp.zeros((PACK_ROWS, tq), F32))
        return jnp.sum(tot, axis=0, keepdims=True)

    def kth_largest16(half_s, need):
        zero = jnp.zeros((1, tq), I32)
        res = jnp.where(count_ge16(half_s, zero) >= need, zero, -2 ** 15)

        def bit_body(b, res):
            cand = res + jnp.left_shift(jnp.int32(1), jnp.int32(14) - b)
            return jnp.where(count_ge16(half_s, cand) >= need, cand, res)

        return lax.fori_loop(0, 15, bit_body, res)

    thr_hi = kth_largest16(hi_s, float(k_top))
    above = jnp.where(thr_hi < 2 ** 15 - 1, count_ge16(hi_s, jnp.minimum(thr_hi + 1, 2 ** 15 - 1)), 0.0)

    def low_chunk(c, _):
        k0 = pl.multiple_of(c * ch, ch)
        key = keys_s[pl.ds(k0, ch), :]
        low = (key & 0xFFFF) - 2 ** 15
        lo_s[pl.ds(k0, ch), :] = jnp.where(jnp.right_shift(key, 16) == thr_hi, low, -2 ** 15).astype(I16)
        return 0

    lax.fori_loop(0, n_valid, low_chunk, 0)
    thr_lo = kth_largest16(lo_s, float(k_top) - above)
    thr = jnp.left_shift(thr_hi, 16) + (thr_lo + 2 ** 15)

    def write_chunk(c, _):
        k0 = pl.multiple_of(c * ch, ch)
        key = keys_s[pl.ds(k0, ch), :]
        o_ref[pl.ds(k0, ch), :] = jnp.where((key >= thr) & (key > KEY_NEG_INF), 0.0, NEG)
        return 0

    def fill_chunk(c, _):
        o_ref[pl.ds(pl.multiple_of(c * ch, ch), ch), :] = jnp.full((ch, tq), NEG, F32)
        return 0

    lax.fori_loop(0, n_valid, write_chunk, 0)
    lax.fori_loop(n_valid, n_ch, fill_chunk, 0)


def _index_select(qh, wt, ki_bf, k_top):
    t = ki_bf.shape[0]
    tq = min(SEL_TQ, t)
    ch = min(SEL_CH, t)
    return pl.pallas_call(
        functools.partial(_index_select_kernel, tq=tq, ch=ch, n_ch=t // ch, k_top=k_top),
        grid=(t // tq,),
        in_specs=[pl.BlockSpec((H_I, tq, DH_I), lambda i: (0, i, 0)),
                  pl.BlockSpec((H_I, tq), lambda i: (0, i)),
                  pl.BlockSpec((t, DH_I), lambda i: (0, 0))],
        out_specs=pl.BlockSpec((t, tq), lambda i: (0, i)),
        out_shape=jax.ShapeDtypeStruct((t, t), F32),
        scratch_shapes=[pltpu.VMEM((t, tq), I32), pltpu.VMEM((t, tq), I16), pltpu.VMEM((t, tq), I16)],
        compiler_params=_params(("arbitrary",)),
        name="index_select",
    )(qh, wt, ki_bf)


def _sample_select_kernel(pt_ref, q_ref, wb_ref, *rest, n_pages, n_out, k_top, n_rep):
    kc_refs, (kn_ref, o_ref, keys_s) = rest[:SEL_PAGES], rest[SEL_PAGES:]
    j = pl.program_id(1)
    n_steps = n_pages // SEL_PAGES
    row = lax.broadcasted_iota(I32, (SROWS, LANES), 0)
    lane = lax.broadcasted_iota(I32, (SROWS, LANES), 1)
    tok = row % N_NEW

    def score(k_f32):
        s = jnp.dot(q_ref[...], k_f32.astype(BF16), preferred_element_type=F32)
        s = jnp.maximum(s, 0.0) * wb_ref[...]
        return jnp.sum(s.reshape(H_I, SROWS, LANES), axis=0)

    @pl.when(j < n_steps)
    def _():
        for g in range(SEL_PAGES):
            keys_s[j * SEL_PAGES + g] = _sort_key(score(kc_refs[g][...]))

    @pl.when(j == n_steps)
    def _():
        val = jnp.where((lane <= tok) & (lane < N_NEW), score(kn_ref[...]), -jnp.inf)
        keys_s[n_pages] = _sort_key(val)
        keys_s[n_pages + 1] = jnp.full((SROWS, LANES), INT_MIN, I32)
        keys = keys_s[...]

        def count_ge(cand):
            hit = (keys >= cand[None]).astype(I32)
            return jnp.sum(jnp.sum(hit, axis=0).astype(F32), axis=1, keepdims=True)

        thr = _kth_largest_key(count_ge, (SROWS, LANES), k_top)
        sel = jnp.where((keys >= thr[None]) & (keys > KEY_NEG_INF), 1.0, 0.0)
        sel = sel.reshape((n_pages + 2) * SROWS, LANES).astype(BF16)
        pos = lax.broadcasted_iota(I32, (LANES, LANES * n_rep), 0)
        rep_lane = lax.broadcasted_iota(I32, (LANES, LANES * n_rep), 1)
        expand = (jnp.right_shift(rep_lane, int(math.log2(n_rep))) == pos).astype(BF16)
        wide = jnp.dot(sel, expand, preferred_element_type=F32)
        o_ref[0:n_pages + 2] = jnp.where(wide > 0.5, 0.0, NEG).reshape(n_pages + 2, SROWS, LANES * n_rep)
        for c in range(n_pages + 2, n_out):
            o_ref[c] = jnp.full((SROWS, LANES * n_rep), NEG, F32)


def _sample_select(page_table, qh, wb, cache_k_idx, ki_new, k_top, n_out, n_rep):
    nb, n_pages = page_table.shape
    assert n_pages % SEL_PAGES == 0 and n_out >= n_pages + 2 and n_rep & (n_rep - 1) == 0

    def page_spec(g):
        return pl.BlockSpec((None, DH_I, PAGE_SIZE),
                            lambda b, j, pt: (pt[b, jnp.minimum(j * SEL_PAGES + g, n_pages - 1)], 0, 0))

    grid_spec = pltpu.PrefetchScalarGridSpec(
        num_scalar_prefetch=1,
        grid=(nb, n_pages // SEL_PAGES + 1),
        in_specs=[pl.BlockSpec((None, H_I * SROWS, DH_I), lambda b, j, pt: (b, 0, 0)),
                  pl.BlockSpec((None, H_I * SROWS, LANES), lambda b, j, pt: (b, 0, 0))]
                 + [page_spec(g) for g in range(SEL_PAGES)]
                 + [pl.BlockSpec((None, DH_I, PAGE_SIZE), lambda b, j, pt: (b, 0, 0))],
        out_specs=pl.BlockSpec((None, n_out, SROWS, LANES * n_rep), lambda b, j, pt: (b, 0, 0, 0)),
        scratch_shapes=[pltpu.VMEM((n_pages + 2, SROWS, LANES), I32)],
    )
    return pl.pallas_call(
        functools.partial(_sample_select_kernel, n_pages=n_pages, n_out=n_out, k_top=k_top, n_rep=n_rep),
        grid_spec=grid_spec,
        out_shape=jax.ShapeDtypeStruct((nb, n_out, SROWS, LANES * n_rep), F32),
        compiler_params=_params(("arbitrary", "arbitrary")),
        name="sample_select",
    )(page_table, qh, wb, *([cache_k_idx] * SEL_PAGES), ki_new)


def _sample_attn_kernel(pt_ref, tabr_ref, lam_ref, sub_ref, q_ref, *rest,
                        n_maps, use_mask, scale, n_pages, n_heads):
    g_n = ATT_PAGES
    kc_refs, vc_refs = rest[:g_n], rest[g_n:2 * g_n]
    rest = rest[2 * g_n:]
    if use_mask:
        kn_ref, vn_ref, m_ref, o_ref, acc, m_s, l_s, far_s = rest
    else:
        kn_ref, vn_ref, o_ref, acc, m_s, l_s, far_s = rest
    j = pl.program_id(1)
    n_steps = n_pages // g_n
    past_len = n_pages * PAGE_SIZE
    rows = n_heads * SROWS
    wide = n_heads * PAGE_SIZE
    head_shift = int(math.log2(n_heads))
    table = lambda b: tabr_ref[:, b:b + 1]

    def geometry():
        row = lax.broadcasted_iota(I32, (rows, wide), 0)
        lane = lax.broadcasted_iota(I32, (rows, wide), 1)
        own = (lane & (n_heads - 1)) == jnp.right_shift(row, int(math.log2(SROWS)))
        return row % N_NEW, jnp.right_shift(lane, head_shift), own

    @pl.when(j == 0)
    def _():
        m_s[...] = jnp.full(m_s.shape, NEG, F32)
        l_s[...] = jnp.zeros(l_s.shape, F32)
        acc[...] = jnp.zeros(acc.shape, F32)
        far_s[...] = jnp.where(geometry()[2], table(NUM_BUCKETS - 1), NEG)

    def update(k_refs, v_refs, biases, page_masks):
        parts = []
        for k_ref, bias, page_mask in zip(k_refs, biases, page_masks):
            s = _dot_nt(q_ref[...], k_ref[...].astype(BF16))
            if scale != 1.0:
                s = s * scale
            s = s + bias
            if page_mask is not None:
                s = s + jnp.concatenate([page_mask] * n_heads, axis=0)
            parts.append(s)
        m_old = m_s[...]
        m_blk = jnp.max(parts[0], axis=1, keepdims=True)
        for s in parts[1:]:
            m_blk = jnp.maximum(m_blk, jnp.max(s, axis=1, keepdims=True))
        m_new = jnp.maximum(m_old, m_blk)
        alpha = jnp.exp(m_old - m_new)
        l_new = alpha * l_s[...]
        acc_new = alpha * acc[...]
        for s, v_ref in zip(parts, v_refs):
            pr = jnp.exp(s - m_new)
            l_new = l_new + jnp.sum(pr, axis=1, keepdims=True)
            acc_new = acc_new + jnp.dot(pr.astype(BF16), v_ref[...].astype(BF16), preferred_element_type=F32)
        l_s[...] = l_new
        acc[...] = acc_new
        m_s[...] = m_new

    first_near = (past_len - T5_FAR - PAGE_SIZE + 1) // PAGE_SIZE + 1
    assert first_near // g_n == n_steps - 1, "only the last cache step may hold near pages"
    step_masks = lambda: [m_ref[g] if use_mask else None for g in range(g_n)]

    @pl.when(j < n_steps - 1)
    def _():
        far = far_s[...]
        update(kc_refs, vc_refs, [far] * g_n, step_masks())

    @pl.when(j == n_steps - 1)
    def _():
        tok, pos, own = geometry()
        biases = []
        for g in range(g_n):
            page = (n_steps - 1) * g_n + g
            if page < first_near:
                biases.append(far_s[...])
            else:
                dist = (past_len + tok) - (page * PAGE_SIZE + pos)
                biases.append(jnp.where(own, _t5_bias(dist, table), NEG))
        update(kc_refs, vc_refs, biases, step_masks())

    @pl.when(j == n_steps)
    def _():
        tok, pos, own = geometry()
        dist = tok - pos
        bias = jnp.where(own & (dist >= 0) & (pos < N_NEW), _t5_bias(dist, table), NEG)
        update([kn_ref], [vn_ref], [bias], [m_ref[0] if use_mask else None])
        o = acc[...] / l_s[...]
        for h in range(n_heads):
            oh = o[h * SROWS:(h + 1) * SROWS]
            if n_maps == 2:
                oh = oh - _lambda_full(lam_ref) * pltpu.roll(oh, N_NEW, 0)
                oh = oh * lax.rsqrt(jnp.mean(oh * oh, axis=1, keepdims=True) + LN_EPS) * sub_ref[...] \
                    * (1.0 - LAMBDA_INIT)
            o_ref[:, h * LANES:(h + 1) * LANES] = oh


def _sample_attn(page_table, tab_rows, lam4, subw, q, cache_k, cache_v, k_new, v_new, mask, *, n_maps, scale, name):
    nb, n_pages = page_table.shape
    assert n_pages % ATT_PAGES == 0
    nh = q.shape[1] // SROWS
    assert nh & (nh - 1) == 0
    slab = PAGE_SIZE * nh

    def page_spec(g):
        return pl.BlockSpec((None, slab, LANES),
                            lambda b, j, pt: (pt[b, jnp.minimum(j * ATT_PAGES + g, n_pages - 1)], 0, 0))

    new_spec = pl.BlockSpec((None, slab, LANES), lambda b, j, pt: (b, 0, 0))
    in_specs = [pl.BlockSpec(tab_rows.shape, lambda b, j, pt: (0, 0)),
                pl.BlockSpec((4, DH_A), lambda b, j, pt: (0, 0)),
                pl.BlockSpec((1, LANES), lambda b, j, pt: (0, 0)),
                pl.BlockSpec((None, nh * SROWS, LANES), lambda b, j, pt: (b, 0, 0))]
    in_specs += [page_spec(g) for g in range(ATT_PAGES)] * 2 + [new_spec, new_spec]
    args = [tab_rows, lam4, subw, q] + [cache_k] * ATT_PAGES + [cache_v] * ATT_PAGES + [k_new, v_new]
    if mask is not None:
        in_specs.append(pl.BlockSpec((None, ATT_PAGES, SROWS, slab), lambda b, j, pt: (b, j, 0, 0)))
        args.append(mask)
    grid_spec = pltpu.PrefetchScalarGridSpec(
        num_scalar_prefetch=1,
        grid=(nb, n_pages // ATT_PAGES + 1),
        in_specs=in_specs,
        out_specs=pl.BlockSpec((None, SROWS, nh * LANES), lambda b, j, pt: (b, 0, 0)),
        scratch_shapes=[pltpu.VMEM((nh * SROWS, LANES), F32), pltpu.VMEM((nh * SROWS, 1), F32),
                        pltpu.VMEM((nh * SROWS, 1), F32), pltpu.VMEM((nh * SROWS, slab), F32)],
    )
    return pl.pallas_call(
        functools.partial(_sample_attn_kernel, n_maps=n_maps, use_mask=mask is not None, scale=scale,
                          n_pages=n_pages, n_heads=nh),
        grid_spec=grid_spec,
        out_shape=jax.ShapeDtypeStruct((nb, SROWS, nh * LANES), F32),
        compiler_params=_params(("arbitrary", "arbitrary")),
        name=name,
    )(page_table, *args)


def _gate_merge_kernel(oa_ref, ob_ref, ga_ref, gb_ref, wa_ref, wb_ref, o_ref):
    a = jnp.dot(oa_ref[...], wa_ref[...], preferred_element_type=F32)
    b = jnp.dot(ob_ref[...], wb_ref[...], preferred_element_type=F32)
    o_ref[...] = (jax.nn.sigmoid(ga_ref[...]) * a + jax.nn.sigmoid(gb_ref[...]) * b).astype(BF16)


def _gate_merge(oa, ob, ga, gb, wa, wb):
    m, d = ga.shape
    tm = min(POST_TM, m)
    row = lambda w: pl.BlockSpec((tm, w), lambda i: (i, 0))
    full = lambda a: pl.BlockSpec(a.shape, lambda i: (0, 0))
    return pl.pallas_call(
        _gate_merge_kernel,
        grid=(m // tm,),
        in_specs=[row(oa.shape[1]), row(ob.shape[1]), row(d), row(d), full(wa), full(wb)],
        out_specs=row(d),
        out_shape=jax.ShapeDtypeStruct((m, d), BF16),
        compiler_params=_params(("arbitrary",)),
        name="gate_merge",
    )(oa, ob, ga, gb, wa, wb)


def _store_token_major(ref, x):
    tm, d = x.shape
    nc = d // LANES
    for c in range(nc):
        ref[pl.ds(c, tm, stride=nc), :] = x[:, c * LANES:(c + 1) * LANES]


def _load_token_major(ref, start, tm, nc, dtype):
    return jnp.concatenate([ref[pl.ds(start + c, tm, stride=nc), :].astype(dtype) for c in range(nc)], axis=1)


def _layer_norm(x, g_ref, b_ref):
    mu = jnp.mean(x, axis=1, keepdims=True)
    xc = x - mu
    var = jnp.mean(xc * xc, axis=1, keepdims=True)
    return xc * lax.rsqrt(var + LN_EPS) * g_ref[...] + b_ref[...]


def _out_ln_route_kernel(mg_ref, x_ref, wo_ref, g_ref, b_ref, wr_ref, br_ref, cin_ref, *rest, tm):
    x1t_ref, route_ref, cnt_ref, carry = rest[-4:]
    i = pl.program_id(0)

    @pl.when(i == 0)
    def _():
        carry[...] = cin_ref[...]

    mix = jnp.dot(mg_ref[...], wo_ref[...], preferred_element_type=F32)
    x1 = _layer_norm(ALPHA * x_ref[...] + mix, g_ref, b_ref)
    _store_token_major(x1t_ref, x1)

    z = jnp.dot(x1, wr_ref[...], preferred_element_type=F32, precision=lax.Precision.HIGHEST) + br_ref[...]
    lane = lax.broadcasted_iota(I32, (tm, LANES), 1).astype(F32)
    ninf = -jnp.inf
    big = jnp.float32(2 ** 30)
    gl = jnp.where(lane < N_GROUPS, z, ninf)
    gmax = jnp.max(gl, axis=1, keepdims=True)
    gsel = jnp.min(jnp.where(gl == gmax, lane, big), axis=1, keepdims=True)
    p_g = 1.0 / jnp.sum(jnp.exp(gl - gmax), axis=1, keepdims=True)
    e_lane = lane - N_GROUPS
    in_grp = (e_lane >= gsel * EXPERTS_PER_GROUP) & (e_lane < (gsel + 1) * EXPERTS_PER_GROUP)
    el = jnp.where(in_grp, z, ninf)
    v1 = jnp.max(el, axis=1, keepdims=True)
    i1 = jnp.min(jnp.where(el == v1, lane, big), axis=1, keepdims=True)
    el2 = jnp.where(lane == i1, ninf, el)
    v2 = jnp.max(el2, axis=1, keepdims=True)
    i2 = jnp.min(jnp.where(el2 == v2, lane, big), axis=1, keepdims=True)
    e2x = jnp.exp(v2 - v1)
    w1 = p_g / (1.0 + e2x)
    w2 = p_g * e2x / (1.0 + e2x)
    e1 = i1 - N_GROUPS
    e2 = i2 - N_GROUPS

    hot1 = lane == e1
    hot2 = lane == e2
    onehot = (hot1 | hot2).astype(BF16)
    r = lax.broadcasted_iota(I32, (tm, tm), 0)
    c = lax.broadcasted_iota(I32, (tm, tm), 1)
    tri = (r > c).astype(BF16)
    prefix = jnp.dot(tri, onehot, preferred_element_type=F32) + carry[...]
    rank1 = jnp.sum(jnp.where(hot1, prefix, 0.0), axis=1, keepdims=True)
    rank2 = jnp.sum(jnp.where(hot2, prefix, 0.0), axis=1, keepdims=True)
    carry[...] = carry[...] + jnp.sum(onehot.astype(F32), axis=0, keepdims=True)
    cnt_ref[...] = carry[...]

    route = jnp.zeros((tm, LANES), F32)
    for j, col in enumerate((e1, e2, w1, w2, rank1, rank2)):
        route = jnp.where(lane == j, col, route)
    route_ref[...] = route


def _out_ln_route(merged, x, wo, g, b, wr, br, carry_in, x1t_prev, tok_off, n_tok_total):
    m, d = x.shape
    tm = min(POST_TM, m)
    nc = d // LANES
    assert tok_off % tm == 0
    blk_off = tok_off // tm
    row = lambda w: pl.BlockSpec((tm, w), lambda i: (i, 0))
    full = lambda a: pl.BlockSpec(a.shape, lambda i: (0, 0))
    sds = jax.ShapeDtypeStruct
    in_specs = [row(d), row(d), full(wo), full(g), full(b), full(wr), full(br), full(carry_in)]
    args = [merged, x, wo, g, b, wr, br, carry_in]
    aliases = {}
    if x1t_prev is not None:
        in_specs.append(pl.BlockSpec(memory_space=pl.ANY))
        args.append(x1t_prev)
        aliases = {len(args) - 1: 0}
    return pl.pallas_call(
        functools.partial(_out_ln_route_kernel, tm=tm),
        grid=(m // tm,),
        in_specs=in_specs,
        out_specs=[pl.BlockSpec((tm * nc, LANES), lambda i: (i + blk_off, 0)), row(LANES),
                   pl.BlockSpec((1, LANES), lambda i: (0, 0))],
        out_shape=[sds((n_tok_total * nc, LANES), F32), sds((m, LANES), F32), sds((1, LANES), F32)],
        scratch_shapes=[pltpu.VMEM((1, LANES), F32)],
        input_output_aliases=aliases,
        compiler_params=_params(("arbitrary",)),
        name="out_ln_route",
    )(*args)


def _inverse_perm_kernel(starts_ref, ea_ref, eb_ref, ra_ref, rb_ref, pos1_ref, pos2_ref, inv_ref, *, n_tok, n_slots):
    def zero(s, _):
        inv_ref[s] = 0
        return 0

    def put(t, _):
        p1 = starts_ref[ea_ref[t]] + ra_ref[t]
        p2 = starts_ref[eb_ref[t]] + rb_ref[t]
        pos1_ref[t] = p1
        pos2_ref[t] = p2
        inv_ref[p1] = t
        inv_ref[p2] = t
        return 0

    lax.fori_loop(0, n_slots, zero, 0, unroll=DMA_UNROLL)
    lax.fori_loop(0, n_tok, put, 0, unroll=DMA_UNROLL)


def _inverse_perm(starts, e1, e2, rank1, rank2, n_slots):
    smem = pl.BlockSpec(memory_space=pltpu.SMEM)
    n_tok = e1.shape[0]
    sds = jax.ShapeDtypeStruct
    return pl.pallas_call(
        functools.partial(_inverse_perm_kernel, n_tok=n_tok, n_slots=n_slots),
        in_specs=[smem] * 5,
        out_specs=[smem] * 3,
        out_shape=[sds((n_tok,), I32), sds((n_tok,), I32), sds((n_slots,), I32)],
        name="inverse_perm",
    )(starts, e1, e2, rank1, rank2)


def _expert_mlp_kernel(te_ref, tv_ref, tf_ref, inv_ref, x_hbm, wg_ref, wu_ref, wd_ref, o_ref,
                       wg_s, wu_s, wd_s, xbuf, sem, *, n_tiles, nc):
    i = pl.program_id(0)
    slab = MOE_TM * nc

    def row_copy(tile, r):
        slot = tile % 2
        src = x_hbm.at[pl.ds(pl.multiple_of(inv_ref[tile * MOE_TM + r] * nc, nc), nc)]
        dst = xbuf.at[pl.ds(pl.multiple_of(slot * slab + r * nc, nc), nc)]
        return pltpu.make_async_copy(src, dst, sem.at[slot])

    def start_gather(tile):
        lax.fori_loop(0, MOE_TM, lambda r, _: (row_copy(tile, r).start(), 0)[1], 0, unroll=DMA_UNROLL)

    def wait_gather(tile):
        lax.fori_loop(0, MOE_TM, lambda r, _: (row_copy(tile, r).wait(), 0)[1], 0, unroll=DMA_UNROLL)

    @pl.when((i == 0) & (tv_ref[0] == 1))
    def _():
        start_gather(0)

    @pl.when((i + 1 < n_tiles) & (tv_ref[jnp.minimum(i + 1, n_tiles - 1)] == 1))
    def _():
        start_gather(i + 1)

    @pl.when(tf_ref[i] == 1)
    def _():
        wg_s[...] = wg_ref[...].astype(BF16)
        wu_s[...] = wu_ref[...].astype(BF16)
        wd_s[...] = wd_ref[...].astype(BF16)

    @pl.when(tv_ref[i] == 1)
    def _():
        wait_gather(i)
        x = _load_token_major(xbuf, (i % 2) * slab, MOE_TM, nc, BF16)
        g = jnp.dot(x, wg_s[...], preferred_element_type=F32)
        u = jnp.dot(x, wu_s[...], preferred_element_type=F32)
        hid = (jax.nn.silu(g) * u).astype(BF16)
        _store_token_major(o_ref, jnp.dot(hid, wd_s[...], preferred_element_type=F32))

    @pl.when(tv_ref[i] == 0)
    def _():
        o_ref[...] = jnp.zeros(o_ref.shape, F32)


def _expert_mlp(tile_expert, tile_valid, tile_first, inv, x1t, w_gate, w_up, w_down):
    d, ff = w_gate.shape[1], w_gate.shape[2]
    nc = d // LANES
    n_tiles = inv.shape[0] // MOE_TM
    grid_spec = pltpu.PrefetchScalarGridSpec(
        num_scalar_prefetch=4, grid=(n_tiles,),
        in_specs=[pl.BlockSpec(memory_space=pl.ANY),
                  pl.BlockSpec((None, d, ff), lambda i, te, tv, tf, inv: (te[i], 0, 0)),
                  pl.BlockSpec((None, d, ff), lambda i, te, tv, tf, inv: (te[i], 0, 0)),
                  pl.BlockSpec((None, ff, d), lambda i, te, tv, tf, inv: (te[i], 0, 0))],
        out_specs=pl.BlockSpec((MOE_TM * nc, LANES), lambda i, te, tv, tf, inv: (i, 0)),
        scratch_shapes=[pltpu.VMEM((d, ff), BF16), pltpu.VMEM((d, ff), BF16), pltpu.VMEM((ff, d), BF16),
                        pltpu.VMEM((2 * MOE_TM * nc, LANES), F32), pltpu.SemaphoreType.DMA((2,))])
    return pl.pallas_call(
        functools.partial(_expert_mlp_kernel, n_tiles=n_tiles, nc=nc),
        grid_spec=grid_spec,
        out_shape=jax.ShapeDtypeStruct((inv.shape[0] * nc, LANES), F32),
        compiler_params=_params(("arbitrary",)),
        name="expert_mlp",
    )(tile_expert, tile_valid, tile_first, inv, x1t, w_gate, w_up, w_down)


def _combine_ln_kernel(pos1_ref, pos2_ref, x1t_ref, route_ref, g_ref, b_ref, ys_hbm, o_ref, g1, g2, sem,
                       *, tm, nc, tok_off):
    base = tok_off + pl.program_id(0) * tm

    def row_copy(r, pos_ref, dst, slot):
        src = ys_hbm.at[pl.ds(pl.multiple_of(pos_ref[base + r] * nc, nc), nc)]
        return pltpu.make_async_copy(src, dst.at[pl.ds(pl.multiple_of(r * nc, nc), nc)], sem.at[slot])

    def start(r, _):
        row_copy(r, pos1_ref, g1, 0).start()
        row_copy(r, pos2_ref, g2, 1).start()
        return 0

    def wait(r, _):
        row_copy(r, pos1_ref, g1, 0).wait()
        row_copy(r, pos2_ref, g2, 1).wait()
        return 0

    lax.fori_loop(0, tm, start, 0, unroll=DMA_UNROLL)
    lax.fori_loop(0, tm, wait, 0, unroll=DMA_UNROLL)
    w1 = route_ref[:, 2:3]
    w2 = route_ref[:, 3:4]
    f = w1 * _load_token_major(g1, 0, tm, nc, F32) + w2 * _load_token_major(g2, 0, tm, nc, F32)
    o_ref[...] = _layer_norm(ALPHA * _load_token_major(x1t_ref, 0, tm, nc, F32) + f, g_ref, b_ref)


def _combine_ln(pos1, pos2, x1t, route, g, b, ys, nc, tok_off):
    m = route.shape[0]
    d = nc * LANES
    tm = min(POST_TM, m)
    assert tok_off % tm == 0
    blk_off = tok_off // tm
    full = lambda a: pl.BlockSpec(a.shape, lambda i, p1, p2: (0, 0))
    grid_spec = pltpu.PrefetchScalarGridSpec(
        num_scalar_prefetch=2, grid=(m // tm,),
        in_specs=[pl.BlockSpec((tm * nc, LANES), lambda i, p1, p2: (i + blk_off, 0)),
                  pl.BlockSpec((tm, LANES), lambda i, p1, p2: (i, 0)),
                  full(g), full(b), pl.BlockSpec(memory_space=pl.ANY)],
        out_specs=pl.BlockSpec((tm, d), lambda i, p1, p2: (i, 0)),
        scratch_shapes=[pltpu.VMEM((tm * nc, LANES), F32), pltpu.VMEM((tm * nc, LANES), F32),
                        pltpu.SemaphoreType.DMA((2,))])
    return pl.pallas_call(
        functools.partial(_combine_ln_kernel, tm=tm, nc=nc, tok_off=tok_off),
        grid_spec=grid_spec,
        out_shape=jax.ShapeDtypeStruct((m, d), F32),
        compiler_params=_params(("arbitrary",)),
        name="combine_ln",
    )(pos1, pos2, x1t, route, g, b, ys)


def kernel(x_prompt, x_sample, cache_k_a, cache_v_a, cache_k_b, cache_v_b, cache_k_idx, page_table, w_in,
           lambda_q1, lambda_k1, lambda_q2, lambda_k2, subln_w, w_br_a, w_br_b, w_o, rel_bias, ln1_g, ln1_b,
           w_router_group, b_router_group, w_router_expert, b_router_expert, w_e_gate, w_e_up, w_e_down,
           ln2_g, ln2_b):
    assert w_in.shape[0] == DEPTH
    bp, t, d = x_prompt.shape
    assert bp == 1
    nb, ts, _ = x_sample.shape
    assert ts == N_NEW
    n_pages = page_table.shape[1]
    past_len = n_pages * PAGE_SIZE
    hd = H_A * 2 * DH_A
    n_phys = cache_k_a.shape[1]
    ms = nb * ts
    nc = d // LANES

    w = w_in[0]
    w_bf = w.astype(BF16)
    s0 = 7 * hd
    w_small = jnp.concatenate([w[:, s0:s0 + DH_I + H_I], jnp.zeros((d, LANES - DH_I - H_I), F32)], axis=1).astype(BF16)
    tab = rel_bias.T.astype(F32)
    lam4 = jnp.stack([lambda_q1[0], lambda_k1[0], lambda_q2[0], lambda_k2[0]]).astype(F32)
    subw = subln_w[0].reshape(1, 2 * DH_A).astype(F32)
    wa_bf = w_br_a[0].astype(BF16)
    wb_bf = w_br_b[0].astype(BF16)
    wo_bf = w_o[0].astype(BF16)
    w_route = jnp.concatenate([w_router_group[0], w_router_expert[0],
                               jnp.zeros((d, LANES - N_GROUPS - N_EXPERTS), F32)], axis=1)
    b_route = jnp.concatenate([b_router_group[0], b_router_expert[0],
                               jnp.zeros((LANES - N_GROUPS - N_EXPERTS,), F32)]).reshape(1, LANES)
    ln1g, ln1b = ln1_g[0].reshape(1, d), ln1_b[0].reshape(1, d)

    xp = x_prompt.reshape(t, d)
    xs = x_sample.reshape(ms, d)
    pp = _in_proj(xp.astype(BF16), w_bf, w_small, True)
    ps = _in_proj(xs.astype(BF16), w_bf, w_small, False)

    oa_p = _flash(pp["qa"], pp["ka_h"], pp["va_t"], None, tab, lam4, subw.reshape(2 * DH_A, 1),
                  n_maps=2, scale=1.0, head_off=0, name="flash_diff")
    qh_p = pp["qi"].reshape(t, H_I, DH_I).transpose(1, 0, 2)
    wt_p = pp["small"][:, DH_I:DH_I + H_I].T
    mask_p = _index_select(qh_p, wt_p, pp["ki_bf"], min(TOPK_MAX, t // 4))
    ob_p = _flash(pp["qb"], pp["kb_h"], pp["vb_t"], mask_p, tab, lam4, subw.reshape(2 * DH_A, 1),
                  n_maps=1, scale=DH_B ** -0.5, head_off=H_A, name="flash_dsa")

    def rows_to_batch(a):
        nh = a.shape[0]
        if a.ndim == 4:
            return a.reshape(nh, 2, nb, ts, LANES).transpose(2, 0, 1, 3, 4).reshape(nb, nh * 2 * ts, LANES)
        a = a.reshape(nh, nb, ts, LANES).transpose(1, 0, 2, 3)
        return jnp.pad(a, ((0, 0), (0, 0), (0, SROWS - ts), (0, 0))).reshape(nb, nh * SROWS, LANES)

    def new_slab(a):
        a = jnp.pad(a.reshape(nb, ts, H_A, LANES), ((0, 0), (0, PAGE_SIZE - ts), (0, 0), (0, 0)))
        return a.reshape(nb, PAGE_SIZE * H_A, LANES)

    def new_rows(a):
        return jnp.pad(a.reshape(nb, ts, a.shape[1]), ((0, 0), (0, PAGE_SIZE - ts), (0, 0)))

    page_view = lambda c: c.reshape(n_phys, PAGE_SIZE * H_A, LANES)
    tab_rows = lambda off: jnp.repeat(tab[off:off + H_A], SROWS, axis=0)
    oa_s = _sample_attn(page_table, tab_rows(0), lam4, subw, rows_to_batch(ps["qa"]),
                        page_view(cache_k_a), page_view(cache_v_a), new_slab(ps["ka"]), new_slab(ps["va"]), None,
                        n_maps=2, scale=1.0, name="sample_diff")
    qh_s = ps["qi"].reshape(nb, ts, H_I, DH_I).transpose(0, 2, 1, 3)
    qh_s = jnp.pad(qh_s, ((0, 0), (0, 0), (0, SROWS - ts), (0, 0))).reshape(nb, H_I * SROWS, DH_I)
    wi_s = ps["small"][:, DH_I:DH_I + H_I].reshape(nb, ts, H_I).transpose(0, 2, 1) * (DH_I ** -0.5 * H_I ** -0.5)
    wi_s = jnp.pad(wi_s, ((0, 0), (0, 0), (0, SROWS - ts))).reshape(nb, H_I * SROWS, 1)
    wb_s = jnp.broadcast_to(wi_s, (nb, H_I * SROWS, LANES))
    kidx_t = jnp.swapaxes(cache_k_idx.reshape(n_phys, PAGE_SIZE, DH_I), 1, 2)
    mask_s = _sample_select(page_table, qh_s, wb_s, kidx_t, jnp.swapaxes(new_rows(ps["ki"]), 1, 2),
                            min(TOPK_MAX, (past_len + ts) // 4), n_pages + ATT_PAGES, H_B)
    ob_s = _sample_attn(page_table, tab_rows(H_A), lam4, subw, rows_to_batch(ps["qb"]),
                        page_view(cache_k_b), page_view(cache_v_b), new_slab(ps["kb"]), new_slab(ps["vb"]), mask_s,
                        n_maps=1, scale=DH_B ** -0.5, name="sample_dsa")
    oa_s = oa_s[:, :ts].reshape(ms, hd).astype(BF16)
    ob_s = ob_s[:, :ts].reshape(ms, hd).astype(BF16)

    n_tok = t + ms
    mg_p = _gate_merge(oa_p, ob_p, pp["ga"], pp["gb"], wa_bf, wb_bf)
    mg_s = _gate_merge(oa_s, ob_s, ps["ga"], ps["gb"], wa_bf, wb_bf)
    x1t, route_p, cnt_p = _out_ln_route(mg_p, xp, wo_bf, ln1g, ln1b, w_route, b_route, jnp.zeros((1, LANES), F32),
                                        jnp.zeros((n_tok * nc, LANES), F32), 0, n_tok)
    x1t, route_s, cnt = _out_ln_route(mg_s, xs, wo_bf, ln1g, ln1b, w_route, b_route, cnt_p, x1t, t, n_tok)
    route = jnp.concatenate([route_p, route_s], axis=0)

    counts = cnt[0, :N_EXPERTS].astype(I32)
    padded = (counts + MOE_TM - 1) // MOE_TM * MOE_TM
    ends = jnp.cumsum(padded)
    starts = ends - padded
    n_tiles = (2 * n_tok + N_EXPERTS * (MOE_TM - 1) + MOE_TM - 1) // MOE_TM
    tile_row = jnp.arange(n_tiles, dtype=I32) * MOE_TM
    tile_expert = jnp.minimum(jnp.sum(tile_row[:, None] >= ends[None, :], axis=1), N_EXPERTS - 1).astype(I32)
    tile_valid = (tile_row < ends[-1]).astype(I32)
    tile_first = ((tile_row == starts[tile_expert]) & (tile_valid == 1)).astype(I32)
    ids = route[:, :SUBLANES].astype(I32)
    pos1, pos2, inv = _inverse_perm(starts.astype(I32), ids[:, 0], ids[:, 1], ids[:, 4], ids[:, 5],
                                    n_tiles * MOE_TM)
    ysort = _expert_mlp(tile_expert, tile_valid, tile_first, inv, x1t, w_e_gate[0], w_e_up[0], w_e_down[0])
    ln2g, ln2b = ln2_g[0].reshape(1, d), ln2_b[0].reshape(1, d)
    y_p = _combine_ln(pos1, pos2, x1t, route_p, ln2g, ln2b, ysort, nc, 0)
    y_s = _combine_ln(pos1, pos2, x1t, route_s, ln2g, ln2b, ysort, nc, t)

    kv5 = lambda a, n: a.reshape(1, n[0], n[1], H_A, 2 * DH_A)
    return (y_p.reshape(1, t, d), y_s.reshape(nb, ts, d),
            kv5(pp["ka"], (1, t)), kv5(pp["va"], (1, t)), kv5(pp["kb"], (1, t)), kv5(pp["vb"], (1, t)),
            pp["ki"].reshape(1, 1, t, DH_I),
            kv5(ps["ka"], (nb, ts)), kv5(ps["va"], (nb, ts)), kv5(ps["kb"], (nb, ts)),
            kv5(ps["vb"], (nb, ts)), ps["ki"].reshape(1, nb, ts, DH_I))
```

```python
import functools
import math

import numpy as np
import jax
import jax.numpy as jnp
from jax import lax
from jax.experimental import pallas as pl
from jax.experimental.pallas import tpu as pltpu

F32 = jnp.float32
BF16 = jnp.bfloat16
I32 = jnp.int32
I16 = jnp.int16

H_A = 8
DH_A = 64
H_B = 8
DH_B = 128
H_I = 16
DH_I = 64
TOPK_MAX = 256
NUM_BUCKETS = 32
MAX_DISTANCE = 128
N_GROUPS = 4
EXPERTS_PER_GROUP = 8
N_EXPERTS = N_GROUPS * EXPERTS_PER_GROUP
PAGE_SIZE = 128
DEPTH = 1
ALPHA = (2 * DEPTH) ** 0.25
LN_EPS = 1e-5
LAMBDA_INIT = 0.8 - 0.6 * math.exp(-0.3 * 0)

LANES = 128
SUBLANES = 8
PACK_ROWS = 16
NEG = -1e30
INT_MIN = -2 ** 31
KEY_NEG_INF = int(np.array([-np.inf], np.float32).view(np.int32)[0]) ^ 0x7FFFFFFF
V7X_VMEM_LIMIT = 48 * 1024 * 1024

PROJ_TM = 512
FLASH_T = 1024
FLASH_SPLIT = 2
VT_ROWS = LANES + 16
SEL_TQ = 256
SEL_CH = 256
POST_TM = 256
MOE_TM = 256
SROWS = 8
N_NEW = 4
ATT_PAGES = 8
SEL_PAGES = 16
DMA_UNROLL = 8


def _t5_thresholds():
    n = np.arange(0, MAX_DISTANCE + 1)
    max_exact = NUM_BUCKETS // 2
    nf = np.maximum(n, 1).astype(np.float32)
    large = max_exact + (np.log(nf / max_exact) / math.log(MAX_DISTANCE / max_exact)
                         * (NUM_BUCKETS - max_exact)).astype(np.int32)
    b = np.where(n < max_exact, n, np.minimum(large, NUM_BUCKETS - 1))
    assert np.all(np.diff(b) >= 0) and b[-1] == NUM_BUCKETS - 1
    return tuple(int(np.argmax(b >= j)) for j in range(1, NUM_BUCKETS))


T5_THRESH = _t5_thresholds()
T5_FAR = T5_THRESH[-1]


def _t5_bias(d, table):
    b = jnp.broadcast_to(table(0), d.shape).astype(F32)
    for j, t in enumerate(T5_THRESH, start=1):
        b = jnp.where(d >= t, table(j), b)
    return b


def _dot_nt(a, b):
    return lax.dot_general(a, b, (((1,), (1,)), ((), ())), preferred_element_type=F32)


def _sort_key(x):
    bits = lax.bitcast_convert_type(x, I32)
    return jnp.where(bits < 0, bits ^ 0x7FFFFFFF, bits)


def _params(sem):
    return pltpu.CompilerParams(dimension_semantics=sem, vmem_limit_bytes=V7X_VMEM_LIMIT)


def _proj_kernel(x_ref, w_ref, *rest, emit, n_extra):
    extra, out_refs = rest[:n_extra], rest[n_extra:]
    res = jnp.dot(x_ref[...], w_ref[...], preferred_element_type=F32)
    emit(res, out_refs, x_ref, *extra)


def _emit_f32(res, outs, x_ref):
    outs[0][...] = res


def _emit_bf16(res, outs, x_ref):
    outs[0][...] = res.astype(BF16)


def _emit_k(res, outs, x_ref):
    outs[0][...] = res
    for h in range(res.shape[1] // LANES):
        outs[1][h] = res[:, h * LANES:(h + 1) * LANES].astype(BF16)


def _emit_v(res, outs, x_ref, wt_ref):
    outs[0][...] = res
    res_t = _dot_nt(wt_ref[...], x_ref[...])
    for h in range(res.shape[1] // LANES):
        outs[1][h, 0:LANES] = res_t[h * LANES:(h + 1) * LANES, :].astype(BF16)
        outs[1][h, LANES:VT_ROWS] = jnp.ones((VT_ROWS - LANES, res_t.shape[1]), BF16)


def _emit_heads(res, outs, x_ref):
    for h in range(res.shape[1] // LANES):
        outs[0][h] = res[:, h * LANES:(h + 1) * LANES].astype(BF16)


def _emit_qa(res, outs, x_ref):
    lane = lax.broadcasted_iota(I32, (res.shape[0], LANES), 1)
    for h in range(res.shape[1] // LANES):
        blk = res[:, h * LANES:(h + 1) * LANES] * (DH_A ** -0.5)
        outs[0][h, 0] = jnp.where(lane < DH_A, blk, 0.0).astype(BF16)
        outs[0][h, 1] = jnp.where(lane >= DH_A, blk, 0.0).astype(BF16)


def _emit_small(res, outs, x_ref):
    outs[0][...] = res
    outs[1][...] = res[:, :DH_I]
    outs[2][...] = res[:, :DH_I].astype(BF16)


def _proj(x, w, emit, out_shapes, out_blocks, tok_axes, name, extra=()):
    m, k = x.shape
    tm = min(PROJ_TM, m)

    def spec(blk, tok_axis):
        nd = len(blk)
        return pl.BlockSpec(blk, lambda i: tuple(i if a == tok_axis else 0 for a in range(nd)))

    full = lambda a: pl.BlockSpec(a.shape, lambda i: (0,) * a.ndim)
    return pl.pallas_call(
        functools.partial(_proj_kernel, emit=emit, n_extra=len(extra)),
        grid=(m // tm,),
        in_specs=[pl.BlockSpec((tm, k), lambda i: (i, 0)), full(w)] + [full(e) for e in extra],
        out_specs=[spec(b, a) for b, a in zip(out_blocks, tok_axes)],
        out_shape=out_shapes,
        compiler_params=_params(("arbitrary",)),
        name=name,
    )(x, w, *extra)


def _in_proj(x, w_bf, w_small_bf, with_attention_layouts):
    m, d = x.shape
    tm = min(PROJ_TM, m)
    hd = H_A * 2 * DH_A
    sds = jax.ShapeDtypeStruct
    cols = lambda j: w_bf[:, j * hd:(j + 1) * hd]
    out = {}
    out["qa"], = _proj(x, cols(0), _emit_qa, [sds((H_A, 2, m, LANES), BF16)], [(H_A, 2, tm, LANES)], [2], "proj_qa")
    for j, nm in ((1, "ka"), (4, "kb")):
        if with_attention_layouts:
            out[nm], out[nm + "_h"] = _proj(x, cols(j), _emit_k, [sds((m, hd), F32), sds((H_A, m, LANES), BF16)],
                                            [(tm, hd), (H_A, tm, LANES)], [0, 1], "proj_" + nm)
        else:
            out[nm], = _proj(x, cols(j), _emit_f32, [sds((m, hd), F32)], [(tm, hd)], [0], "proj_" + nm)
    for j, nm in ((2, "va"), (5, "vb")):
        if with_attention_layouts:
            out[nm], out[nm + "_t"] = _proj(x, cols(j), _emit_v, [sds((m, hd), F32), sds((H_A, VT_ROWS, m), BF16)],
                                            [(tm, hd), (H_A, VT_ROWS, tm)], [0, 2], "proj_" + nm,
                                            extra=(cols(j).T,))
        else:
            out[nm], = _proj(x, cols(j), _emit_f32, [sds((m, hd), F32)], [(tm, hd)], [0], "proj_" + nm)
    out["qb"], = _proj(x, cols(3), _emit_heads, [sds((H_B, m, LANES), BF16)], [(H_B, tm, LANES)], [1], "proj_qb")
    out["qi"], = _proj(x, cols(6), _emit_bf16, [sds((m, hd), BF16)], [(tm, hd)], [0], "proj_qi")
    g0 = 7 * hd + DH_I + H_I
    out["ga"], = _proj(x, w_bf[:, g0:g0 + d], _emit_f32, [sds((m, d), F32)], [(tm, d)], [0], "proj_ga")
    out["gb"], = _proj(x, w_bf[:, g0 + d:g0 + 2 * d], _emit_f32, [sds((m, d), F32)], [(tm, d)], [0], "proj_gb")
    out["small"], out["ki"], out["ki_bf"] = _proj(
        x, w_small_bf, _emit_small, [sds((m, LANES), F32), sds((m, DH_I), F32), sds((m, DH_I), BF16)],
        [(tm, LANES), (tm, DH_I), (tm, DH_I)], [0, 0, 0], "proj_small")
    return out


def _lambda_full(lam_ref):
    a = jnp.sum(lam_ref[0:1, :] * lam_ref[1:2, :], axis=1, keepdims=True)
    b = jnp.sum(lam_ref[2:3, :] * lam_ref[3:4, :], axis=1, keepdims=True)
    return jnp.exp(a) - jnp.exp(b) + LAMBDA_INIT


def _flash_kernel(qs_ref, ks_ref, tab_ref, lam_ref, sub_ref, q_ref, k_ref, vt_ref, *rest,
                  n_maps, tq, use_mask, scale, head_off):
    if use_mask:
        m_ref, o_ref, acc, m_s, bias_s = rest
    else:
        o_ref, acc, m_s, bias_s = rest
    h = pl.program_id(0)
    step = pl.program_id(1)
    qi = qs_ref[step]
    ki = ks_ref[step]
    cols = n_maps * tq
    hb = h + head_off
    table = lambda j: tab_ref[hb, j]

    @pl.when(step == 0)
    def _():
        r = lax.broadcasted_iota(I32, (tq, tq), 0)
        c = lax.broadcasted_iota(I32, (tq, tq), 1)
        d0 = c - r
        bias_s[0] = jnp.where(d0 >= 0, _t5_bias(d0, table), NEG)
        bias_s[1] = _t5_bias(d0 + tq, table)

    @pl.when(ki == 0)
    def _():
        m_s[...] = jnp.full(m_s.shape, NEG, F32)
        acc[...] = jnp.zeros(acc.shape, F32)

    def update(bias, uniform):
        q = q_ref[...].reshape(cols, LANES)
        kp = tq // FLASH_SPLIT
        parts = []
        for part in range(FLASH_SPLIT):
            rows = slice(part * kp, (part + 1) * kp)
            s = _dot_nt(k_ref[rows, :], q)
            if scale != 1.0:
                s = s * scale
            if not uniform:
                s = s + jnp.concatenate([bias[rows]] * n_maps, axis=1)
            if use_mask:
                s = s + m_ref[rows, :].astype(F32)
            parts.append(s)
        m_old = m_s[...]
        m_blk = jnp.max(parts[0], axis=0, keepdims=True)
        for s in parts[1:]:
            m_blk = jnp.maximum(m_blk, jnp.max(s, axis=0, keepdims=True))
        if uniform:
            m_blk = m_blk + bias
        m_new = jnp.maximum(m_old, m_blk)
        alpha = jnp.exp(m_old - m_new)
        shift = (m_new - bias) if uniform else m_new
        acc_new = alpha * acc[...]
        for part, s in enumerate(parts):
            p = jnp.exp(s - shift).astype(BF16)
            acc_new = acc_new + jnp.dot(vt_ref[:, part * kp:(part + 1) * kp], p, preferred_element_type=F32)
        acc[...] = acc_new
        m_s[...] = m_new

    @pl.when(qi - ki >= 2)
    def _():
        update(tab_ref[hb, NUM_BUCKETS - 1], True)

    @pl.when(qi - ki < 2)
    def _():
        update(bias_s[qi - ki], False)

    @pl.when(ki == qi)
    def _():
        o = acc[0:LANES, :] / acc[LANES:LANES + 1, :]
        if n_maps == 2:
            o = o[:, :tq] - _lambda_full(lam_ref) * o[:, tq:]
            o = o * lax.rsqrt(jnp.mean(o * o, axis=0, keepdims=True) + LN_EPS) * sub_ref[...] * (1.0 - LAMBDA_INIT)
        o_ref[...] = o.T.astype(BF16)


def _flash(q, k, vt, mask_t, tab, lam4, subw_col, *, n_maps, scale, head_off, name):
    nh, t = k.shape[0], k.shape[1]
    tq = min(FLASH_T, t)
    nq = t // tq
    pairs = [(a, b) for a in range(nq) for b in range(a + 1)]
    qs = jnp.asarray([p[0] for p in pairs], I32)
    ks = jnp.asarray([p[1] for p in pairs], I32)
    cols = n_maps * tq
    smem = pl.BlockSpec(memory_space=pltpu.SMEM)
    if n_maps == 2:
        q_spec = pl.BlockSpec((None, 2, tq, LANES), lambda h, s, qs, ks: (h, 0, qs[s], 0))
    else:
        q_spec = pl.BlockSpec((None, tq, LANES), lambda h, s, qs, ks: (h, qs[s], 0))
    in_specs = [smem, pl.BlockSpec((4, DH_A), lambda h, s, qs, ks: (0, 0)),
                pl.BlockSpec((LANES, 1), lambda h, s, qs, ks: (0, 0)), q_spec,
                pl.BlockSpec((None, tq, LANES), lambda h, s, qs, ks: (h, ks[s], 0)),
                pl.BlockSpec((None, VT_ROWS, tq), lambda h, s, qs, ks: (h, 0, ks[s]))]
    args = [tab, lam4, subw_col, q, k, vt]
    if mask_t is not None:
        in_specs.append(pl.BlockSpec((tq, tq), lambda h, s, qs, ks: (ks[s], qs[s])))
        args.append(mask_t)
    grid_spec = pltpu.PrefetchScalarGridSpec(
        num_scalar_prefetch=2,
        grid=(nh, len(pairs)),
        in_specs=in_specs,
        out_specs=pl.BlockSpec((tq, LANES), lambda h, s, qs, ks: (qs[s], h)),
        scratch_shapes=[pltpu.VMEM((VT_ROWS, cols), F32), pltpu.VMEM((1, cols), F32),
                        pltpu.VMEM((2, tq, tq), F32)],
    )
    return pl.pallas_call(
        functools.partial(_flash_kernel, n_maps=n_maps, tq=tq, use_mask=mask_t is not None, scale=scale,
                          head_off=head_off),
        grid_spec=grid_spec,
        out_shape=jax.ShapeDtypeStruct((t, nh * LANES), BF16),
        compiler_params=_params(("arbitrary", "arbitrary")),
        name=name,
    )(qs, ks, *args)


def _kth_largest_key(count_ge, shape, k_top):
    cand0 = jnp.zeros(shape, I32)
    res = jnp.where(count_ge(cand0) >= k_top, cand0, jnp.full(shape, INT_MIN, I32))

    def bit_body(b, res):
        cand = res + jnp.left_shift(jnp.int32(1), jnp.int32(30) - b)
        return jnp.where(count_ge(cand) >= k_top, cand, res)

    return lax.fori_loop(0, 31, bit_body, res)


def _index_select_kernel(qh_ref, wt_ref, ki_ref, o_ref, keys_s, hi_s, lo_s, *, tq, ch, n_ch, k_top):
    i = pl.program_id(0)
    q0 = i * tq
    n_valid = (q0 + tq + ch - 1) // ch
    wt = wt_ref[...] * (DH_I ** -0.5 * H_I ** -0.5)
    krow = lax.broadcasted_iota(I32, (ch, tq), 0)
    qpos = lax.broadcasted_iota(I32, (ch, tq), 1) + q0

    def score_chunk(c, _):
        k0 = pl.multiple_of(c * ch, ch)
        kc = ki_ref[pl.ds(k0, ch), :]
        acc = jnp.zeros((ch, tq), F32)
        for h in range(H_I):
            acc = acc + wt[h:h + 1, :] * jnp.maximum(_dot_nt(kc, qh_ref[h]), 0.0)
        key = _sort_key(jnp.where(krow + k0 <= qpos, acc, -jnp.inf))
        keys_s[pl.ds(k0, ch), :] = key
        hi_s[pl.ds(k0, ch), :] = jnp.right_shift(key, 16).astype(I16)
        return 0

    lax.fori_loop(0, n_valid, score_chunk, 0)

    def count_ge16(half_s, cand):
        c16 = cand.astype(I16)

        def chunk(c, tot):
            blk = half_s[pl.ds(pl.multiple_of(c * ch, ch), ch), :]
            hit = jnp.where(blk >= c16, jnp.ones((), BF16), jnp.zeros((), BF16))
            parts = [hit[r * PACK_ROWS:(r + 1) * PACK_ROWS] for r in range(ch // PACK_ROWS)]
            while len(parts) > 1:
                parts = [parts[a] + parts[a + 1] for a in range(0, len(parts), 2)]
            return tot + parts[0].astype(F32)

        tot = lax.fori_loop(0, n_valid, chunk, jnp.zeros((PACK_ROWS, tq), F32))
        return jnp.sum(tot, axis=0, keepdims=True)

    def kth_largest16(half_s, need):
        zero = jnp.zeros((1, tq), I32)
        res = jnp.where(count_ge16(half_s, zero) >= need, zero, -2 ** 15)

        def bit_body(b, res):
            cand = res + jnp.left_shift(jnp.int32(1), jnp.int32(14) - b)
            return jnp.where(count_ge16(half_s, cand) >= need, cand, res)

        return lax.fori_loop(0, 15, bit_body, res)

    thr_hi = kth_largest16(hi_s, float(k_top))
    above = jnp.where(thr_hi < 2 ** 15 - 1, count_ge16(hi_s, jnp.minimum(thr_hi + 1, 2 ** 15 - 1)), 0.0)

    def low_chunk(c, _):
        k0 = pl.multiple_of(c * ch, ch)
        key = keys_s[pl.ds(k0, ch), :]
        low = (key & 0xFFFF) - 2 ** 15
        lo_s[pl.ds(k0, ch), :] = jnp.where(jnp.right_shift(key, 16) == thr_hi, low, -2 ** 15).astype(I16)
        return 0

    lax.fori_loop(0, n_valid, low_chunk, 0)
    thr_lo = kth_largest16(lo_s, float(k_top) - above)
    thr = jnp.left_shift(thr_hi, 16) + (thr_lo + 2 ** 15)

    def write_chunk(c, _):
        k0 = pl.multiple_of(c * ch, ch)
        key = keys_s[pl.ds(k0, ch), :]
        o_ref[pl.ds(k0, ch), :] = jnp.where((key >= thr) & (key > KEY_NEG_INF), 0.0, NEG).astype(BF16)
        return 0

    def fill_chunk(c, _):
        o_ref[pl.ds(pl.multiple_of(c * ch, ch), ch), :] = jnp.full((ch, tq), NEG, BF16)
        return 0

    lax.fori_loop(0, n_valid, write_chunk, 0)
    lax.fori_loop(n_valid, n_ch, fill_chunk, 0)


def _index_select(qh, wt, ki_bf, k_top):
    t = ki_bf.shape[0]
    tq = min(SEL_TQ, t)
    ch = min(SEL_CH, t)
    return pl.pallas_call(
        functools.partial(_index_select_kernel, tq=tq, ch=ch, n_ch=t // ch, k_top=k_top),
        grid=(t // tq,),
        in_specs=[pl.BlockSpec((H_I, tq, DH_I), lambda i: (0, i, 0)),
                  pl.BlockSpec((H_I, tq), lambda i: (0, i)),
                  pl.BlockSpec((t, DH_I), lambda i: (0, 0))],
        out_specs=pl.BlockSpec((t, tq), lambda i: (0, i)),
        out_shape=jax.ShapeDtypeStruct((t, t), BF16),
        scratch_shapes=[pltpu.VMEM((t, tq), I32), pltpu.VMEM((t, tq), I16), pltpu.VMEM((t, tq), I16)],
        compiler_params=_params(("arbitrary",)),
        name="index_select",
    )(qh, wt, ki_bf)


def _sample_select_kernel(pt_ref, q_ref, wb_ref, *rest, n_pages, n_out, k_top, n_rep):
    kc_refs, (kn_ref, o_ref, keys_s) = rest[:SEL_PAGES], rest[SEL_PAGES:]
    j = pl.program_id(1)
    n_steps = n_pages // SEL_PAGES
    row = lax.broadcasted_iota(I32, (SROWS, LANES), 0)
    lane = lax.broadcasted_iota(I32, (SROWS, LANES), 1)
    tok = row % N_NEW

    def score(k_f32):
        s = jnp.dot(q_ref[...], k_f32.astype(BF16), preferred_element_type=F32)
        s = jnp.maximum(s, 0.0) * wb_ref[...]
        return jnp.sum(s.reshape(H_I, SROWS, LANES), axis=0)

    @pl.when(j < n_steps)
    def _():
        for g in range(SEL_PAGES):
            keys_s[j * SEL_PAGES + g] = _sort_key(score(kc_refs[g][...]))

    @pl.when(j == n_steps)
    def _():
        val = jnp.where((lane <= tok) & (lane < N_NEW), score(kn_ref[...]), -jnp.inf)
        keys_s[n_pages] = _sort_key(val)
        keys_s[n_pages + 1] = jnp.full((SROWS, LANES), INT_MIN, I32)
        keys = keys_s[...]

        def count_ge(cand):
            hit = (keys >= cand[None]).astype(I32)
            return jnp.sum(jnp.sum(hit, axis=0).astype(F32), axis=1, keepdims=True)

        thr = _kth_largest_key(count_ge, (SROWS, LANES), k_top)
        sel = jnp.where((keys >= thr[None]) & (keys > KEY_NEG_INF), 1.0, 0.0)
        sel = sel.reshape((n_pages + 2) * SROWS, LANES).astype(BF16)
        pos = lax.broadcasted_iota(I32, (LANES, LANES * n_rep), 0)
        rep_lane = lax.broadcasted_iota(I32, (LANES, LANES * n_rep), 1)
        expand = (jnp.right_shift(rep_lane, int(math.log2(n_rep))) == pos).astype(BF16)
        wide = jnp.dot(sel, expand, preferred_element_type=F32)
        o_ref[0:n_pages + 2] = jnp.where(wide > 0.5, 0.0, NEG).reshape(n_pages + 2, SROWS, LANES * n_rep)
        for c in range(n_pages + 2, n_out):
            o_ref[c] = jnp.full((SROWS, LANES * n_rep), NEG, F32)


def _sample_select(page_table, qh, wb, cache_k_idx, ki_new, k_top, n_out, n_rep):
    nb, n_pages = page_table.shape
    assert n_pages % SEL_PAGES == 0 and n_out >= n_pages + 2 and n_rep & (n_rep - 1) == 0

    def page_spec(g):
        return pl.BlockSpec((None, DH_I, PAGE_SIZE),
                            lambda b, j, pt: (pt[b, jnp.minimum(j * SEL_PAGES + g, n_pages - 1)], 0, 0))

    grid_spec = pltpu.PrefetchScalarGridSpec(
        num_scalar_prefetch=1,
        grid=(nb, n_pages // SEL_PAGES + 1),
        in_specs=[pl.BlockSpec((None, H_I * SROWS, DH_I), lambda b, j, pt: (b, 0, 0)),
                  pl.BlockSpec((None, H_I * SROWS, LANES), lambda b, j, pt: (b, 0, 0))]
                 + [page_spec(g) for g in range(SEL_PAGES)]
                 + [pl.BlockSpec((None, DH_I, PAGE_SIZE), lambda b, j, pt: (b, 0, 0))],
        out_specs=pl.BlockSpec((None, n_out, SROWS, LANES * n_rep), lambda b, j, pt: (b, 0, 0, 0)),
        scratch_shapes=[pltpu.VMEM((n_pages + 2, SROWS, LANES), I32)],
    )
    return pl.pallas_call(
        functools.partial(_sample_select_kernel, n_pages=n_pages, n_out=n_out, k_top=k_top, n_rep=n_rep),
        grid_spec=grid_spec,
        out_shape=jax.ShapeDtypeStruct((nb, n_out, SROWS, LANES * n_rep), F32),
        compiler_params=_params(("arbitrary", "arbitrary")),
        name="sample_select",
    )(page_table, qh, wb, *([cache_k_idx] * SEL_PAGES), ki_new)


def _sample_attn_kernel(pt_ref, tabr_ref, lam_ref, sub_ref, q_ref, *rest,
                        n_maps, use_mask, scale, n_pages, n_heads):
    g_n = ATT_PAGES
    kc_refs, vc_refs = rest[:g_n], rest[g_n:2 * g_n]
    rest = rest[2 * g_n:]
    if use_mask:
        kn_ref, vn_ref, m_ref, o_ref, acc, m_s, l_s, far_s = rest
    else:
        kn_ref, vn_ref, o_ref, acc, m_s, l_s, far_s = rest
    j = pl.program_id(1)
    n_steps = n_pages // g_n
    past_len = n_pages * PAGE_SIZE
    rows = n_heads * SROWS
    wide = n_heads * PAGE_SIZE
    head_shift = int(math.log2(n_heads))
    table = lambda b: tabr_ref[:, b:b + 1]

    def geometry():
        row = lax.broadcasted_iota(I32, (rows, wide), 0)
        lane = lax.broadcasted_iota(I32, (rows, wide), 1)
        own = (lane & (n_heads - 1)) == jnp.right_shift(row, int(math.log2(SROWS)))
        return row % N_NEW, jnp.right_shift(lane, head_shift), own

    @pl.when(j == 0)
    def _():
        m_s[...] = jnp.full(m_s.shape, NEG, F32)
        l_s[...] = jnp.zeros(l_s.shape, F32)
        acc[...] = jnp.zeros(acc.shape, F32)
        far_s[...] = jnp.where(geometry()[2], table(NUM_BUCKETS - 1), NEG)

    def update(k_refs, v_refs, biases, page_masks):
        parts = []
        for k_ref, bias, page_mask in zip(k_refs, biases, page_masks):
            s = _dot_nt(q_ref[...], k_ref[...].astype(BF16))
            if scale != 1.0:
                s = s * scale
            s = s + bias
            if page_mask is not None:
                s = s + jnp.concatenate([page_mask] * n_heads, axis=0)
            parts.append(s)
        m_old = m_s[...]
        m_blk = jnp.max(parts[0], axis=1, keepdims=True)
        for s in parts[1:]:
            m_blk = jnp.maximum(m_blk, jnp.max(s, axis=1, keepdims=True))
        m_new = jnp.maximum(m_old, m_blk)
        alpha = jnp.exp(m_old - m_new)
        l_new = alpha * l_s[...]
        acc_new = alpha * acc[...]
        for s, v_ref in zip(parts, v_refs):
            pr = jnp.exp(s - m_new)
            l_new = l_new + jnp.sum(pr, axis=1, keepdims=True)
            acc_new = acc_new + jnp.dot(pr.astype(BF16), v_ref[...].astype(BF16), preferred_element_type=F32)
        l_s[...] = l_new
        acc[...] = acc_new
        m_s[...] = m_new

    first_near = (past_len - T5_FAR - PAGE_SIZE + 1) // PAGE_SIZE + 1
    assert first_near // g_n == n_steps - 1, "only the last cache step may hold near pages"
    step_masks = lambda: [m_ref[g] if use_mask else None for g in range(g_n)]

    @pl.when(j < n_steps - 1)
    def _():
        far = far_s[...]
        update(kc_refs, vc_refs, [far] * g_n, step_masks())

    @pl.when(j == n_steps - 1)
    def _():
        tok, pos, own = geometry()
        biases = []
        for g in range(g_n):
            page = (n_steps - 1) * g_n + g
            if page < first_near:
                biases.append(far_s[...])
            else:
                dist = (past_len + tok) - (page * PAGE_SIZE + pos)
                biases.append(jnp.where(own, _t5_bias(dist, table), NEG))
        update(kc_refs, vc_refs, biases, step_masks())

    @pl.when(j == n_steps)
    def _():
        tok, pos, own = geometry()
        dist = tok - pos
        bias = jnp.where(own & (dist >= 0) & (pos < N_NEW), _t5_bias(dist, table), NEG)
        update([kn_ref], [vn_ref], [bias], [m_ref[0] if use_mask else None])
        o = acc[...] / l_s[...]
        for h in range(n_heads):
            oh = o[h * SROWS:(h + 1) * SROWS]
            if n_maps == 2:
                oh = oh - _lambda_full(lam_ref) * pltpu.roll(oh, N_NEW, 0)
                oh = oh * lax.rsqrt(jnp.mean(oh * oh, axis=1, keepdims=True) + LN_EPS) * sub_ref[...] \
                    * (1.0 - LAMBDA_INIT)
            o_ref[:, h * LANES:(h + 1) * LANES] = oh


def _sample_attn(page_table, tab_rows, lam4, subw, q, cache_k, cache_v, k_new, v_new, mask, *, n_maps, scale, name):
    nb, n_pages = page_table.shape
    assert n_pages % ATT_PAGES == 0
    nh = q.shape[1] // SROWS
    assert nh & (nh - 1) == 0
    slab = PAGE_SIZE * nh

    def page_spec(g):
        return pl.BlockSpec((None, slab, LANES),
                            lambda b, j, pt: (pt[b, jnp.minimum(j * ATT_PAGES + g, n_pages - 1)], 0, 0))

    new_spec = pl.BlockSpec((None, slab, LANES), lambda b, j, pt: (b, 0, 0))
    in_specs = [pl.BlockSpec(tab_rows.shape, lambda b, j, pt: (0, 0)),
                pl.BlockSpec((4, DH_A), lambda b, j, pt: (0, 0)),
                pl.BlockSpec((1, LANES), lambda b, j, pt: (0, 0)),
                pl.BlockSpec((None, nh * SROWS, LANES), lambda b, j, pt: (b, 0, 0))]
    in_specs += [page_spec(g) for g in range(ATT_PAGES)] * 2 + [new_spec, new_spec]
    args = [tab_rows, lam4, subw, q] + [cache_k] * ATT_PAGES + [cache_v] * ATT_PAGES + [k_new, v_new]
    if mask is not None:
        in_specs.append(pl.BlockSpec((None, ATT_PAGES, SROWS, slab), lambda b, j, pt: (b, j, 0, 0)))
        args.append(mask)
    grid_spec = pltpu.PrefetchScalarGridSpec(
        num_scalar_prefetch=1,
        grid=(nb, n_pages // ATT_PAGES + 1),
        in_specs=in_specs,
        out_specs=pl.BlockSpec((None, SROWS, nh * LANES), lambda b, j, pt: (b, 0, 0)),
        scratch_shapes=[pltpu.VMEM((nh * SROWS, LANES), F32), pltpu.VMEM((nh * SROWS, 1), F32),
                        pltpu.VMEM((nh * SROWS, 1), F32), pltpu.VMEM((nh * SROWS, slab), F32)],
    )
    return pl.pallas_call(
        functools.partial(_sample_attn_kernel, n_maps=n_maps, use_mask=mask is not None, scale=scale,
                          n_pages=n_pages, n_heads=nh),
        grid_spec=grid_spec,
        out_shape=jax.ShapeDtypeStruct((nb, SROWS, nh * LANES), F32),
        compiler_params=_params(("arbitrary", "arbitrary")),
        name=name,
    )(page_table, *args)


def _gate_merge_kernel(oa_ref, ob_ref, ga_ref, gb_ref, wa_ref, wb_ref, o_ref):
    a = jnp.dot(oa_ref[...], wa_ref[...], preferred_element_type=F32)
    b = jnp.dot(ob_ref[...], wb_ref[...], preferred_element_type=F32)
    o_ref[...] = (jax.nn.sigmoid(ga_ref[...]) * a + jax.nn.sigmoid(gb_ref[...]) * b).astype(BF16)


def _gate_merge(oa, ob, ga, gb, wa, wb):
    m, d = ga.shape
    tm = min(POST_TM, m)
    row = lambda w: pl.BlockSpec((tm, w), lambda i: (i, 0))
    full = lambda a: pl.BlockSpec(a.shape, lambda i: (0, 0))
    return pl.pallas_call(
        _gate_merge_kernel,
        grid=(m // tm,),
        in_specs=[row(oa.shape[1]), row(ob.shape[1]), row(d), row(d), full(wa), full(wb)],
        out_specs=row(d),
        out_shape=jax.ShapeDtypeStruct((m, d), BF16),
        compiler_params=_params(("arbitrary",)),
        name="gate_merge",
    )(oa, ob, ga, gb, wa, wb)


def _store_token_major(ref, x):
    tm, d = x.shape
    nc = d // LANES
    for c in range(nc):
        ref[pl.ds(c, tm, stride=nc), :] = x[:, c * LANES:(c + 1) * LANES]


def _load_token_major(ref, start, tm, nc, dtype):
    return jnp.concatenate([ref[pl.ds(start + c, tm, stride=nc), :].astype(dtype) for c in range(nc)], axis=1)


def _layer_norm(x, g_ref, b_ref):
    mu = jnp.mean(x, axis=1, keepdims=True)
    xc = x - mu
    var = jnp.mean(xc * xc, axis=1, keepdims=True)
    return xc * lax.rsqrt(var + LN_EPS) * g_ref[...] + b_ref[...]


def _out_ln_route_kernel(mg_ref, x_ref, wo_ref, g_ref, b_ref, wr_ref, br_ref, cin_ref, *rest, tm):
    x1t_ref, route_ref, cnt_ref, carry = rest[-4:]
    i = pl.program_id(0)

    @pl.when(i == 0)
    def _():
        carry[...] = cin_ref[...]

    mix = jnp.dot(mg_ref[...], wo_ref[...], preferred_element_type=F32)
    x1 = _layer_norm(ALPHA * x_ref[...] + mix, g_ref, b_ref)
    _store_token_major(x1t_ref, x1)

    z = jnp.dot(x1, wr_ref[...], preferred_element_type=F32, precision=lax.Precision.HIGHEST) + br_ref[...]
    lane = lax.broadcasted_iota(I32, (tm, LANES), 1).astype(F32)
    ninf = -jnp.inf
    big = jnp.float32(2 ** 30)
    gl = jnp.where(lane < N_GROUPS, z, ninf)
    gmax = jnp.max(gl, axis=1, keepdims=True)
    gsel = jnp.min(jnp.where(gl == gmax, lane, big), axis=1, keepdims=True)
    p_g = 1.0 / jnp.sum(jnp.exp(gl - gmax), axis=1, keepdims=True)
    e_lane = lane - N_GROUPS
    in_grp = (e_lane >= gsel * EXPERTS_PER_GROUP) & (e_lane < (gsel + 1) * EXPERTS_PER_GROUP)
    el = jnp.where(in_grp, z, ninf)
    v1 = jnp.max(el, axis=1, keepdims=True)
    i1 = jnp.min(jnp.where(el == v1, lane, big), axis=1, keepdims=True)
    el2 = jnp.where(lane == i1, ninf, el)
    v2 = jnp.max(el2, axis=1, keepdims=True)
    i2 = jnp.min(jnp.where(el2 == v2, lane, big), axis=1, keepdims=True)
    e2x = jnp.exp(v2 - v1)
    w1 = p_g / (1.0 + e2x)
    w2 = p_g * e2x / (1.0 + e2x)
    e1 = i1 - N_GROUPS
    e2 = i2 - N_GROUPS

    hot1 = lane == e1
    hot2 = lane == e2
    onehot = (hot1 | hot2).astype(BF16)
    r = lax.broadcasted_iota(I32, (tm, tm), 0)
    c = lax.broadcasted_iota(I32, (tm, tm), 1)
    tri = (r > c).astype(BF16)
    prefix = jnp.dot(tri, onehot, preferred_element_type=F32) + carry[...]
    rank1 = jnp.sum(jnp.where(hot1, prefix, 0.0), axis=1, keepdims=True)
    rank2 = jnp.sum(jnp.where(hot2, prefix, 0.0), axis=1, keepdims=True)
    carry[...] = carry[...] + jnp.sum(onehot.astype(F32), axis=0, keepdims=True)
    cnt_ref[...] = carry[...]

    route = jnp.zeros((tm, LANES), F32)
    for j, col in enumerate((e1, e2, w1, w2, rank1, rank2)):
        route = jnp.where(lane == j, col, route)
    route_ref[...] = route


def _out_ln_route(merged, x, wo, g, b, wr, br, carry_in, x1t_prev, tok_off, n_tok_total):
    m, d = x.shape
    tm = min(POST_TM, m)
    nc = d // LANES
    assert tok_off % tm == 0
    blk_off = tok_off // tm
    row = lambda w: pl.BlockSpec((tm, w), lambda i: (i, 0))
    full = lambda a: pl.BlockSpec(a.shape, lambda i: (0, 0))
    sds = jax.ShapeDtypeStruct
    in_specs = [row(d), row(d), full(wo), full(g), full(b), full(wr), full(br), full(carry_in)]
    args = [merged, x, wo, g, b, wr, br, carry_in]
    aliases = {}
    if x1t_prev is not None:
        in_specs.append(pl.BlockSpec(memory_space=pl.ANY))
        args.append(x1t_prev)
        aliases = {len(args) - 1: 0}
    return pl.pallas_call(
        functools.partial(_out_ln_route_kernel, tm=tm),
        grid=(m // tm,),
        in_specs=in_specs,
        out_specs=[pl.BlockSpec((tm * nc, LANES), lambda i: (i + blk_off, 0)), row(LANES),
                   pl.BlockSpec((1, LANES), lambda i: (0, 0))],
        out_shape=[sds((n_tok_total * nc, LANES), F32), sds((m, LANES), F32), sds((1, LANES), F32)],
        scratch_shapes=[pltpu.VMEM((1, LANES), F32)],
        input_output_aliases=aliases,
        compiler_params=_params(("arbitrary",)),
        name="out_ln_route",
    )(*args)


def _inverse_perm_kernel(starts_ref, ea_ref, eb_ref, ra_ref, rb_ref, pos1_ref, pos2_ref, inv_ref, *, n_tok, n_slots):
    def zero(s, _):
        inv_ref[s] = 0
        return 0

    def put(t, _):
        p1 = starts_ref[ea_ref[t]] + ra_ref[t]
        p2 = starts_ref[eb_ref[t]] + rb_ref[t]
        pos1_ref[t] = p1
        pos2_ref[t] = p2
        inv_ref[p1] = t
        inv_ref[p2] = t
        return 0

    lax.fori_loop(0, n_slots, zero, 0, unroll=DMA_UNROLL)
    lax.fori_loop(0, n_tok, put, 0, unroll=DMA_UNROLL)


def _inverse_perm(starts, e1, e2, rank1, rank2, n_slots):
    smem = pl.BlockSpec(memory_space=pltpu.SMEM)
    n_tok = e1.shape[0]
    sds = jax.ShapeDtypeStruct
    return pl.pallas_call(
        functools.partial(_inverse_perm_kernel, n_tok=n_tok, n_slots=n_slots),
        in_specs=[smem] * 5,
        out_specs=[smem] * 3,
        out_shape=[sds((n_tok,), I32), sds((n_tok,), I32), sds((n_slots,), I32)],
        name="inverse_perm",
    )(starts, e1, e2, rank1, rank2)


def _expert_mlp_kernel(te_ref, tv_ref, tf_ref, inv_ref, x_hbm, wg_ref, wu_ref, wd_ref, o_ref,
                       wg_s, wu_s, wd_s, xbuf, sem, *, n_tiles, nc):
    i = pl.program_id(0)
    slab = MOE_TM * nc

    def row_copy(tile, r):
        slot = tile % 2
        src = x_hbm.at[pl.ds(pl.multiple_of(inv_ref[tile * MOE_TM + r] * nc, nc), nc)]
        dst = xbuf.at[pl.ds(pl.multiple_of(slot * slab + r * nc, nc), nc)]
        return pltpu.make_async_copy(src, dst, sem.at[slot])

    def start_gather(tile):
        lax.fori_loop(0, MOE_TM, lambda r, _: (row_copy(tile, r).start(), 0)[1], 0, unroll=DMA_UNROLL)

    def wait_gather(tile):
        lax.fori_loop(0, MOE_TM, lambda r, _: (row_copy(tile, r).wait(), 0)[1], 0, unroll=DMA_UNROLL)

    @pl.when((i == 0) & (tv_ref[0] == 1))
    def _():
        start_gather(0)

    @pl.when((i + 1 < n_tiles) & (tv_ref[jnp.minimum(i + 1, n_tiles - 1)] == 1))
    def _():
        start_gather(i + 1)

    @pl.when(tf_ref[i] == 1)
    def _():
        wg_s[...] = wg_ref[...].astype(BF16)
        wu_s[...] = wu_ref[...].astype(BF16)
        wd_s[...] = wd_ref[...].astype(BF16)

    @pl.when(tv_ref[i] == 1)
    def _():
        wait_gather(i)
        x = _load_token_major(xbuf, (i % 2) * slab, MOE_TM, nc, BF16)
        g = jnp.dot(x, wg_s[...], preferred_element_type=F32)
        u = jnp.dot(x, wu_s[...], preferred_element_type=F32)
        hid = (jax.nn.silu(g) * u).astype(BF16)
        _store_token_major(o_ref, jnp.dot(hid, wd_s[...], preferred_element_type=F32))

    @pl.when(tv_ref[i] == 0)
    def _():
        o_ref[...] = jnp.zeros(o_ref.shape, F32)


def _expert_mlp(tile_expert, tile_valid, tile_first, inv, x1t, w_gate, w_up, w_down):
    d, ff = w_gate.shape[1], w_gate.shape[2]
    nc = d // LANES
    n_tiles = inv.shape[0] // MOE_TM
    grid_spec = pltpu.PrefetchScalarGridSpec(
        num_scalar_prefetch=4, grid=(n_tiles,),
        in_specs=[pl.BlockSpec(memory_space=pl.ANY),
                  pl.BlockSpec((None, d, ff), lambda i, te, tv, tf, inv: (te[i], 0, 0)),
                  pl.BlockSpec((None, d, ff), lambda i, te, tv, tf, inv: (te[i], 0, 0)),
                  pl.BlockSpec((None, ff, d), lambda i, te, tv, tf, inv: (te[i], 0, 0))],
        out_specs=pl.BlockSpec((MOE_TM * nc, LANES), lambda i, te, tv, tf, inv: (i, 0)),
        scratch_shapes=[pltpu.VMEM((d, ff), BF16), pltpu.VMEM((d, ff), BF16), pltpu.VMEM((ff, d), BF16),
                        pltpu.VMEM((2 * MOE_TM * nc, LANES), F32), pltpu.SemaphoreType.DMA((2,))])
    return pl.pallas_call(
        functools.partial(_expert_mlp_kernel, n_tiles=n_tiles, nc=nc),
        grid_spec=grid_spec,
        out_shape=jax.ShapeDtypeStruct((inv.shape[0] * nc, LANES), F32),
        compiler_params=_params(("arbitrary",)),
        name="expert_mlp",
    )(tile_expert, tile_valid, tile_first, inv, x1t, w_gate, w_up, w_down)


def _combine_ln_kernel(pos1_ref, pos2_ref, x1t_ref, route_ref, g_ref, b_ref, ys_hbm, o_ref, g1, g2, sem,
                       *, tm, nc, tok_off):
    base = tok_off + pl.program_id(0) * tm

    def row_copy(r, pos_ref, dst, slot):
        src = ys_hbm.at[pl.ds(pl.multiple_of(pos_ref[base + r] * nc, nc), nc)]
        return pltpu.make_async_copy(src, dst.at[pl.ds(pl.multiple_of(r * nc, nc), nc)], sem.at[slot])

    def start(r, _):
        row_copy(r, pos1_ref, g1, 0).start()
        row_copy(r, pos2_ref, g2, 1).start()
        return 0

    def wait(r, _):
        row_copy(r, pos1_ref, g1, 0).wait()
        row_copy(r, pos2_ref, g2, 1).wait()
        return 0

    lax.fori_loop(0, tm, start, 0, unroll=DMA_UNROLL)
    lax.fori_loop(0, tm, wait, 0, unroll=DMA_UNROLL)
    w1 = route_ref[:, 2:3]
    w2 = route_ref[:, 3:4]
    f = w1 * _load_token_major(g1, 0, tm, nc, F32) + w2 * _load_token_major(g2, 0, tm, nc, F32)
    o_ref[...] = _layer_norm(ALPHA * _load_token_major(x1t_ref, 0, tm, nc, F32) + f, g_ref, b_ref)


def _combine_ln(pos1, pos2, x1t, route, g, b, ys, nc, tok_off):
    m = route.shape[0]
    d = nc * LANES
    tm = min(POST_TM, m)
    assert tok_off % tm == 0
    blk_off = tok_off // tm
    full = lambda a: pl.BlockSpec(a.shape, lambda i, p1, p2: (0, 0))
    grid_spec = pltpu.PrefetchScalarGridSpec(
        num_scalar_prefetch=2, grid=(m // tm,),
        in_specs=[pl.BlockSpec((tm * nc, LANES), lambda i, p1, p2: (i + blk_off, 0)),
                  pl.BlockSpec((tm, LANES), lambda i, p1, p2: (i, 0)),
                  full(g), full(b), pl.BlockSpec(memory_space=pl.ANY)],
        out_specs=pl.BlockSpec((tm, d), lambda i, p1, p2: (i, 0)),
        scratch_shapes=[pltpu.VMEM((tm * nc, LANES), F32), pltpu.VMEM((tm * nc, LANES), F32),
                        pltpu.SemaphoreType.DMA((2,))])
    return pl.pallas_call(
        functools.partial(_combine_ln_kernel, tm=tm, nc=nc, tok_off=tok_off),
        grid_spec=grid_spec,
        out_shape=jax.ShapeDtypeStruct((m, d), F32),
        compiler_params=_params(("arbitrary",)),
        name="combine_ln",
    )(pos1, pos2, x1t, route, g, b, ys)


def kernel(x_prompt, x_sample, cache_k_a, cache_v_a, cache_k_b, cache_v_b, cache_k_idx, page_table, w_in,
           lambda_q1, lambda_k1, lambda_q2, lambda_k2, subln_w, w_br_a, w_br_b, w_o, rel_bias, ln1_g, ln1_b,
           w_router_group, b_router_group, w_router_expert, b_router_expert, w_e_gate, w_e_up, w_e_down,
           ln2_g, ln2_b):
    assert w_in.shape[0] == DEPTH
    bp, t, d = x_prompt.shape
    assert bp == 1
    nb, ts, _ = x_sample.shape
    assert ts == N_NEW
    n_pages = page_table.shape[1]
    past_len = n_pages * PAGE_SIZE
    hd = H_A * 2 * DH_A
    n_phys = cache_k_a.shape[1]
    ms = nb * ts
    nc = d // LANES

    w = w_in[0]
    w_bf = w.astype(BF16)
    s0 = 7 * hd
    w_small = jnp.concatenate([w[:, s0:s0 + DH_I + H_I], jnp.zeros((d, LANES - DH_I - H_I), F32)], axis=1).astype(BF16)
    tab = rel_bias.T.astype(F32)
    lam4 = jnp.stack([lambda_q1[0], lambda_k1[0], lambda_q2[0], lambda_k2[0]]).astype(F32)
    subw = subln_w[0].reshape(1, 2 * DH_A).astype(F32)
    wa_bf = w_br_a[0].astype(BF16)
    wb_bf = w_br_b[0].astype(BF16)
    wo_bf = w_o[0].astype(BF16)
    w_route = jnp.concatenate([w_router_group[0], w_router_expert[0],
                               jnp.zeros((d, LANES - N_GROUPS - N_EXPERTS), F32)], axis=1)
    b_route = jnp.concatenate([b_router_group[0], b_router_expert[0],
                               jnp.zeros((LANES - N_GROUPS - N_EXPERTS,), F32)]).reshape(1, LANES)
    ln1g, ln1b = ln1_g[0].reshape(1, d), ln1_b[0].reshape(1, d)

    xp = x_prompt.reshape(t, d)
    xs = x_sample.reshape(ms, d)
    pp = _in_proj(xp.astype(BF16), w_bf, w_small, True)
    ps = _in_proj(xs.astype(BF16), w_bf, w_small, False)

    oa_p = _flash(pp["qa"], pp["ka_h"], pp["va_t"], None, tab, lam4, subw.reshape(2 * DH_A, 1),
                  n_maps=2, scale=1.0, head_off=0, name="flash_diff")
    qh_p = pp["qi"].reshape(t, H_I, DH_I).transpose(1, 0, 2)
    wt_p = pp["small"][:, DH_I:DH_I + H_I].T
    mask_p = _index_select(qh_p, wt_p, pp["ki_bf"], min(TOPK_MAX, t // 4))
    ob_p = _flash(pp["qb"], pp["kb_h"], pp["vb_t"], mask_p, tab, lam4, subw.reshape(2 * DH_A, 1),
                  n_maps=1, scale=DH_B ** -0.5, head_off=H_A, name="flash_dsa")

    def rows_to_batch(a):
        nh = a.shape[0]
        if a.ndim == 4:
            return a.reshape(nh, 2, nb, ts, LANES).transpose(2, 0, 1, 3, 4).reshape(nb, nh * 2 * ts, LANES)
        a = a.reshape(nh, nb, ts, LANES).transpose(1, 0, 2, 3)
        return jnp.pad(a, ((0, 0), (0, 0), (0, SROWS - ts), (0, 0))).reshape(nb, nh * SROWS, LANES)

    def new_slab(a):
        a = jnp.pad(a.reshape(nb, ts, H_A, LANES), ((0, 0), (0, PAGE_SIZE - ts), (0, 0), (0, 0)))
        return a.reshape(nb, PAGE_SIZE * H_A, LANES)

    def new_rows(a):
        return jnp.pad(a.reshape(nb, ts, a.shape[1]), ((0, 0), (0, PAGE_SIZE - ts), (0, 0)))

    page_view = lambda c: c.reshape(n_phys, PAGE_SIZE * H_A, LANES)
    tab_rows = lambda off: jnp.repeat(tab[off:off + H_A], SROWS, axis=0)
    oa_s = _sample_attn(page_table, tab_rows(0), lam4, subw, rows_to_batch(ps["qa"]),
                        page_view(cache_k_a), page_view(cache_v_a), new_slab(ps["ka"]), new_slab(ps["va"]), None,
                        n_maps=2, scale=1.0, name="sample_diff")
    qh_s = ps["qi"].reshape(nb, ts, H_I, DH_I).transpose(0, 2, 1, 3)
    qh_s = jnp.pad(qh_s, ((0, 0), (0, 0), (0, SROWS - ts), (0, 0))).reshape(nb, H_I * SROWS, DH_I)
    wi_s = ps["small"][:, DH_I:DH_I + H_I].reshape(nb, ts, H_I).transpose(0, 2, 1) * (DH_I ** -0.5 * H_I ** -0.5)
    wi_s = jnp.pad(wi_s, ((0, 0), (0, 0), (0, SROWS - ts))).reshape(nb, H_I * SROWS, 1)
    wb_s = jnp.broadcast_to(wi_s, (nb, H_I * SROWS, LANES))
    kidx_t = jnp.swapaxes(cache_k_idx.reshape(n_phys, PAGE_SIZE, DH_I), 1, 2)
    mask_s = _sample_select(page_table, qh_s, wb_s, kidx_t, jnp.swapaxes(new_rows(ps["ki"]), 1, 2),
                            min(TOPK_MAX, (past_len + ts) // 4), n_pages + ATT_PAGES, H_B)
    ob_s = _sample_attn(page_table, tab_rows(H_A), lam4, subw, rows_to_batch(ps["qb"]),
                        page_view(cache_k_b), page_view(cache_v_b), new_slab(ps["kb"]), new_slab(ps["vb"]), mask_s,
                        n_maps=1, scale=DH_B ** -0.5, name="sample_dsa")
    oa_s = oa_s[:, :ts].reshape(ms, hd).astype(BF16)
    ob_s = ob_s[:, :ts].reshape(ms, hd).astype(BF16)

    n_tok = t + ms
    mg_p = _gate_merge(oa_p, ob_p, pp["ga"], pp["gb"], wa_bf, wb_bf)
    mg_s = _gate_merge(oa_s, ob_s, ps["ga"], ps["gb"], wa_bf, wb_bf)
    x1t, route_p, cnt_p = _out_ln_route(mg_p, xp, wo_bf, ln1g, ln1b, w_route, b_route, jnp.zeros((1, LANES), F32),
                                        jnp.zeros((n_tok * nc, LANES), F32), 0, n_tok)
    x1t, route_s, cnt = _out_ln_route(mg_s, xs, wo_bf, ln1g, ln1b, w_route, b_route, cnt_p, x1t, t, n_tok)
    route = jnp.concatenate([route_p, route_s], axis=0)

    counts = cnt[0, :N_EXPERTS].astype(I32)
    padded = (counts + MOE_TM - 1) // MOE_TM * MOE_TM
    ends = jnp.cumsum(padded)
    starts = ends - padded
    n_tiles = (2 * n_tok + N_EXPERTS * (MOE_TM - 1) + MOE_TM - 1) // MOE_TM
    tile_row = jnp.arange(n_tiles, dtype=I32) * MOE_TM
    tile_expert = jnp.minimum(jnp.sum(tile_row[:, None] >= ends[None, :], axis=1), N_EXPERTS - 1).astype(I32)
    tile_valid = (tile_row < ends[-1]).astype(I32)
    tile_first = ((tile_row == starts[tile_expert]) & (tile_valid == 1)).astype(I32)
    ids = route[:, :SUBLANES].astype(I32)
    pos1, pos2, inv = _inverse_perm(starts.astype(I32), ids[:, 0], ids[:, 1], ids[:, 4], ids[:, 5],
                                    n_tiles * MOE_TM)
    ysort = _expert_mlp(tile_expert, tile_valid, tile_first, inv, x1t, w_e_gate[0], w_e_up[0], w_e_down[0])
    ln2g, ln2b = ln2_g[0].reshape(1, d), ln2_b[0].reshape(1, d)
    y_p = _combine_ln(pos1, pos2, x1t, route_p, ln2g, ln2b, ysort, nc, 0)
    y_s = _combine_ln(pos1, pos2, x1t, route_s, ln2g, ln2b, ysort, nc, t)

    kv5 = lambda a, n: a.reshape(1, n[0], n[1], H_A, 2 * DH_A)
    return (y_p.reshape(1, t, d), y_s.reshape(nb, ts, d),
            kv5(pp["ka"], (1, t)), kv5(pp["va"], (1, t)), kv5(pp["kb"], (1, t)), kv5(pp["vb"], (1, t)),
            pp["ki"].reshape(1, 1, t, DH_I),
            kv5(ps["ka"], (nb, ts)), kv5(ps["va"], (nb, ts)), kv5(ps["kb"], (nb, ts)),
            kv5(ps["vb"], (nb, ts)), ps["ki"].reshape(1, nb, ts, DH_I))
```
